```python
import math
import jax, jax.numpy as jnp
from jax import lax
import numpy as np

D_MODEL = 4096
BATCH = 1
SEQ = 16384
DEPTH = 1
DEC_BATCH = 8
DEC_SEQ = 2048
PAST_LEN = 128

HEAD_DIM = 128
GRID_W = 64
Q_BLOCK = 128
ROPE_THETA = 10000.0
RMS_EPS = 1e-6
A_HEADS = 16
A_KV_HEADS = 4
A_GROUP = A_HEADS // A_KV_HEADS
A_WIDTH = A_HEADS * HEAD_DIM
A_KV_WIDTH = A_KV_HEADS * HEAD_DIM
B_HEADS = 8
B_WIDTH = B_HEADS * 2 * HEAD_DIM
MIX_WIDTH = A_WIDTH + B_WIDTH
IN_COLS = A_WIDTH + 2 * A_KV_WIDTH + 3 * B_WIDTH
N_EXPERTS = 32
TOP_K = 4
D_FF = D_MODEL
SWIGLU_LIMIT = 7.0
SWIGLU_ALPHA = 1.702
MOE_BLOCK = 512
N_MOD = 6

kernel_name = "hybrid_gqa_diffattn_moe_encoder"


def rmsnorm(x, g):
    xf = x.astype(jnp.float32)
    y = xf * lax.rsqrt(jnp.mean(xf * xf, axis=-1, keepdims=True) + RMS_EPS)
    return (y * g.astype(jnp.float32)).astype(x.dtype)


def grid_positions(S):
    n_rows = S // GRID_W
    row = jnp.repeat(jnp.arange(n_rows, dtype=jnp.float32), GRID_W)
    col = jnp.tile(jnp.arange(GRID_W, dtype=jnp.float32), n_rows)
    return row, col


def axial_rope(x, row, col):
    half = HEAD_DIM // 2
    inv = ROPE_THETA ** (-jnp.arange(0, half, 2, dtype=jnp.float32) / half)
    xf = x.astype(jnp.float32)

    def rot(xp, pos):
        ang = pos[:, None] * inv[None, :]
        cos = jnp.cos(ang)[None, :, None, :]
        sin = jnp.sin(ang)[None, :, None, :]
        x1, x2 = xp[..., : half // 2], xp[..., half // 2:]
        return jnp.concatenate([x1 * cos - x2 * sin, x2 * cos + x1 * sin], axis=-1)

    out = jnp.concatenate([rot(xf[..., :half], row), rot(xf[..., half:], col)], axis=-1)
    return out.astype(x.dtype)


def gqa_attention(q, k, v):
    B, S = q.shape[0], q.shape[1]
    nb = S // Q_BLOCK
    qb = (q * (HEAD_DIM ** -0.5)).reshape(B, nb, Q_BLOCK, A_KV_HEADS, A_GROUP, HEAD_DIM)
    qb = qb.transpose(1, 0, 3, 4, 2, 5)

    def one(qi):
        s = jnp.einsum('bkgqd,bskd->bkgqs', qi, k).astype(jnp.float32)
        p = jax.nn.softmax(s, axis=-1)
        return jnp.einsum('bkgqs,bskd->bkgqd', p.astype(v.dtype), v)

    o = lax.map(one, qb)
    return o.transpose(1, 0, 4, 2, 3, 5).reshape(B, S, A_WIDTH)


def diff_attention(q, k, v, lam):
    B, S = q.shape[0], q.shape[1]
    nb = S // Q_BLOCK
    slopes = 2.0 ** (-8.0 * jnp.arange(1, B_HEADS + 1, dtype=jnp.float32) / B_HEADS)
    qb = (q * (HEAD_DIM ** -0.5)).reshape(B, nb, Q_BLOCK, B_HEADS, 2, HEAD_DIM)
    qb = qb.transpose(1, 0, 4, 3, 2, 5)
    starts = jnp.arange(nb, dtype=jnp.int32) * Q_BLOCK
    kpos = jnp.arange(S, dtype=jnp.float32)

    def one(args):
        qi, st = args
        s = jnp.einsum('bnhqd,bshnd->bnhqs', qi, k).astype(jnp.float32)
        qpos = (st + jnp.arange(Q_BLOCK, dtype=jnp.int32)).astype(jnp.float32)
        bias = -slopes[:, None, None] * jnp.abs(qpos[:, None] - kpos[None, :])
        p = jax.nn.softmax(s + bias[None, None], axis=-1)
        attn = p[:, 0] - lam * p[:, 1]
        return jnp.einsum('bhqs,bshe->bhqe', attn.astype(v.dtype), v)

    o = lax.map(one, (qb, starts))
    return o.transpose(1, 0, 3, 2, 4).reshape(B, S, B_HEADS, 2 * HEAD_DIM)


def token_mixer(h, w_in, w_out, q_norm, k_norm, lam_p, subln, lam_init):
    B, S, _ = h.shape
    proj = jnp.einsum('bsd,de->bse', h, w_in)
    cuts = [A_WIDTH, A_WIDTH + A_KV_WIDTH, A_WIDTH + 2 * A_KV_WIDTH,
            A_WIDTH + 2 * A_KV_WIDTH + B_WIDTH, A_WIDTH + 2 * A_KV_WIDTH + 2 * B_WIDTH]
    qa, ka, va, qb, kb, vb = jnp.split(proj, cuts, axis=-1)
    row, col = grid_positions(S)
    qa = axial_rope(rmsnorm(qa.reshape(B, S, A_HEADS, HEAD_DIM), q_norm), row, col)
    ka = axial_rope(rmsnorm(ka.reshape(B, S, A_KV_HEADS, HEAD_DIM), k_norm), row, col)
    va = va.reshape(B, S, A_KV_HEADS, HEAD_DIM)
    oa = gqa_attention(qa, ka, va)
    lp = lam_p.astype(jnp.float32)
    lam = jnp.exp(jnp.sum(lp[0] * lp[1])) - jnp.exp(jnp.sum(lp[2] * lp[3])) + lam_init
    qb = qb.reshape(B, S, B_HEADS, 2, HEAD_DIM)
    kb = kb.reshape(B, S, B_HEADS, 2, HEAD_DIM)
    vb = vb.reshape(B, S, B_HEADS, 2 * HEAD_DIM)
    ob = diff_attention(qb, kb, vb, lam)
    ob = (rmsnorm(ob, subln) * (1.0 - lam_init)).reshape(B, S, B_WIDTH)
    return jnp.einsum('bse,ed->bsd', jnp.concatenate([oa, ob], axis=-1), w_out)


def moe(h, w_router, b_router, w_gu, b_gu, w_down, b_down):
    N, D = h.shape
    logits = h.astype(jnp.float32) @ w_router.astype(jnp.float32) + b_router.astype(jnp.float32)
    top_val, top_idx = lax.top_k(logits, TOP_K)
    top_w = jax.nn.softmax(top_val, axis=-1)
    A = N * TOP_K
    flat_e = top_idx.reshape(-1).astype(jnp.int32)
    flat_tok = jnp.repeat(jnp.arange(N, dtype=jnp.int32), TOP_K)
    flat_w = top_w.reshape(-1)
    order = jnp.argsort(flat_e)
    s_e, s_tok, s_w = flat_e[order], flat_tok[order], flat_w[order]
    counts = jnp.bincount(flat_e, length=N_EXPERTS).astype(jnp.int32)
    padded = ((counts + MOE_BLOCK - 1) // MOE_BLOCK) * MOE_BLOCK
    ends = jnp.cumsum(padded)
    pad_start = ends - padded
    start = jnp.cumsum(counts) - counts
    dest = pad_start[s_e] + (jnp.arange(A, dtype=jnp.int32) - start[s_e])
    n_blocks = (A + MOE_BLOCK - 1) // MOE_BLOCK + N_EXPERTS
    total = n_blocks * MOE_BLOCK
    slot_tok = jnp.zeros((total,), jnp.int32).at[dest].set(s_tok)
    slot_w = jnp.zeros((total,), jnp.float32).at[dest].set(s_w)
    block_starts = jnp.arange(n_blocks, dtype=jnp.int32) * MOE_BLOCK
    block_e = jnp.clip(jnp.searchsorted(ends, block_starts, side='right'), 0, N_EXPERTS - 1)

    def expert_block(args):
        tok, wt, e = args
        xb = h[tok]
        gu = (xb @ w_gu[e] + b_gu[e]).astype(jnp.float32)
        g, u = gu[:, :D_FF], gu[:, D_FF:]
        g = jnp.minimum(g, SWIGLU_LIMIT)
        u = jnp.clip(u, -SWIGLU_LIMIT, SWIGLU_LIMIT)
        act = ((u + 1.0) * (g * jax.nn.sigmoid(SWIGLU_ALPHA * g))).astype(h.dtype)
        out = (act @ w_down[e] + b_down[e]).astype(jnp.float32)
        return (out * wt[:, None]).astype(h.dtype)

    outs = lax.map(expert_block, (slot_tok.reshape(n_blocks, MOE_BLOCK),
                                  slot_w.reshape(n_blocks, MOE_BLOCK), block_e))
    return jnp.zeros_like(h).at[slot_tok].add(outs.reshape(total, D))


def trunk(x, c, weights):
    (w_ada, b_ada, g_pre_mix, g_post_mix, g_pre_ffn, g_post_ffn, w_in, w_out, q_norm, k_norm,
     lam_params, subln, w_router, b_router, w_gate_up, b_gate_up, w_down, b_down) = weights
    B, S, D = x.shape
    for l in range(DEPTH):
        lam_init = 0.8 - 0.6 * math.exp(-0.3 * l)
        mod = jax.nn.silu(c.astype(jnp.float32)) @ w_ada[l].astype(jnp.float32) + b_ada[l].astype(jnp.float32)
        mod = mod.reshape(B, N_MOD, D)[:, :, None, :].astype(x.dtype)
        sh_m, sc_m, gt_m, sh_f, sc_f, gt_f = [mod[:, i] for i in range(N_MOD)]
        h = rmsnorm(x, g_pre_mix[l]) * (1 + sc_m) + sh_m
        mix = token_mixer(h, w_in[l], w_out[l], q_norm[l], k_norm[l], lam_params[l], subln[l], lam_init)
        x = x + gt_m * rmsnorm(mix, g_post_mix[l])
        h = rmsnorm(x, g_pre_ffn[l]) * (1 + sc_f) + sh_f
        f = moe(h.reshape(B * S, D), w_router[l], b_router[l], w_gate_up[l], b_gate_up[l],
                w_down[l], b_down[l]).reshape(B, S, D)
        x = x + gt_f * rmsnorm(f, g_post_ffn[l])
    return x


def setup_inputs(seed: int = 0) -> dict:
    key = jax.random.key(seed)
    ks = jax.random.split(key, 24)
    f32 = jnp.float32
    nrm = lambda k, shape, s: jax.random.normal(k, shape, f32) * s
    gain = lambda k, shape: 1.0 + 0.02 * jax.random.normal(k, shape, f32)
    return {
        "x_prompt": nrm(ks[0], (BATCH, SEQ, D_MODEL), 1.0),
        "x_sample": nrm(ks[1], (DEC_BATCH, DEC_SEQ, D_MODEL), 1.0),
        "c_prompt": nrm(ks[2], (BATCH, D_MODEL), 1.0),
        "c_sample": nrm(ks[3], (DEC_BATCH, D_MODEL), 1.0),
        "w_ada": nrm(ks[4], (DEPTH, D_MODEL, N_MOD * D_MODEL), 0.5 * D_MODEL ** -0.5),
        "b_ada": nrm(ks[5], (DEPTH, N_MOD * D_MODEL), 0.02),
        "g_pre_mix": gain(ks[6], (DEPTH, D_MODEL)),
        "g_post_mix": gain(ks[7], (DEPTH, D_MODEL)),
        "g_pre_ffn": gain(ks[8], (DEPTH, D_MODEL)),
        "g_post_ffn": gain(ks[9], (DEPTH, D_MODEL)),
        "w_in": nrm(ks[10], (DEPTH, D_MODEL, IN_COLS), D_MODEL ** -0.5),
        "w_out": nrm(ks[11], (DEPTH, MIX_WIDTH, D_MODEL), MIX_WIDTH ** -0.5),
        "q_norm": gain(ks[12], (DEPTH, HEAD_DIM)),
        "k_norm": gain(ks[13], (DEPTH, HEAD_DIM)),
        "lam_params": nrm(ks[14], (DEPTH, 4, HEAD_DIM), 0.1),
        "subln": gain(ks[15], (DEPTH, 2 * HEAD_DIM)),
        "w_router": nrm(ks[16], (DEPTH, D_MODEL, N_EXPERTS), D_MODEL ** -0.5),
        "b_router": nrm(ks[17], (DEPTH, N_EXPERTS), 0.01),
        "w_gate_up": nrm(ks[18], (DEPTH, N_EXPERTS, D_MODEL, 2 * D_FF), D_MODEL ** -0.5),
        "b_gate_up": nrm(ks[19], (DEPTH, N_EXPERTS, 2 * D_FF), 0.01),
        "w_down": nrm(ks[20], (DEPTH, N_EXPERTS, D_FF, D_MODEL), D_FF ** -0.5),
        "b_down": nrm(ks[21], (DEPTH, N_EXPERTS, D_MODEL), 0.01),
    }


def reference(x_prompt, x_sample, c_prompt, c_sample, w_ada, b_ada, g_pre_mix, g_post_mix,
              g_pre_ffn, g_post_ffn, w_in, w_out, q_norm, k_norm, lam_params, subln,
              w_router, b_router, w_gate_up, b_gate_up, w_down, b_down):
    weights = (w_ada, b_ada, g_pre_mix, g_post_mix, g_pre_ffn, g_post_ffn, w_in, w_out, q_norm,
               k_norm, lam_params, subln, w_router, b_router, w_gate_up, b_gate_up, w_down, b_down)
    y_prompt = trunk(x_prompt, c_prompt, weights)
    y_sample = trunk(x_sample, c_sample, weights)
    return (y_prompt, y_sample)
```

```python
import functools
import math

import jax
import jax.numpy as jnp
from jax import lax
from jax.experimental import pallas as pl
from jax.experimental.pallas import tpu as pltpu

F32 = jnp.float32
BF16 = jnp.bfloat16
I32 = jnp.int32
U32 = jnp.uint32

HEAD_DIM = 128
GRID_W = 64
ROPE_THETA = 10000.0
RMS_EPS = 1e-6
A_HEADS = 16
A_KV_HEADS = 4
B_HEADS = 8
N_EXPERTS = 32
TOP_K = 4
SWIGLU_LIMIT = 7.0
SWIGLU_ALPHA = 1.702
N_MOD = 6
LAM_INIT = 0.8 - 0.6 * math.exp(-0.3 * 0)

LANES = 128
V7X_VMEM_BYTES = 64 * 1024 * 1024
VMEM_LIMIT = 56 * 1024 * 1024
NEG_BIG = -1e30

INPROJ_BM = 512
ATTN_A_BQ = 256
ATTN_A_BK = 512
ATTN_B_BQ = 512
ATTN_B_BK = 512
OUTMM_BM = 1024
OUTMM_BN = 512
POST_BM = 256
ROUTE_BT = 512
MOE_BM = 512
GU_TF = 512
DOWN_BN = 1024
DISPATCH_BT = 256
FINAL_BT = 128
ADA_BN = 512


def _cparams(sem):
    return pltpu.CompilerParams(dimension_semantics=("arbitrary",) * len(sem), vmem_limit_bytes=VMEM_LIMIT)


def _ada_kernel(c_ref, w_ref, b_ref, o_ref):
    c = c_ref[...]
    a = (c * jax.nn.sigmoid(c)).astype(BF16)
    o_ref[...] = jnp.dot(a, w_ref[...].astype(BF16), preferred_element_type=F32) + b_ref[...]


def _ada(c_pad, w_ada, b_ada):
    rows, d = c_pad.shape
    n = w_ada.shape[1]
    bn = min(ADA_BN, n)
    return pl.pallas_call(
        _ada_kernel,
        grid=(n // bn,),
        in_specs=[pl.BlockSpec((rows, d), lambda j: (0, 0)),
                  pl.BlockSpec((d, bn), lambda j: (0, j)),
                  pl.BlockSpec((1, bn), lambda j: (0, j))],
        out_specs=pl.BlockSpec((rows, bn), lambda j: (0, j)),
        out_shape=jax.ShapeDtypeStruct((rows, n), F32),
        compiler_params=_cparams(("parallel",)),
        name="ada",
    )(c_pad, w_ada, b_ada.reshape(1, n))


def _swap_pairs(x):
    n = x.shape[-1]
    lane = lax.broadcasted_iota(I32, x.shape, x.ndim - 1)
    up = pltpu.roll(x, n - 32, x.ndim - 1)
    dn = pltpu.roll(x, 32, x.ndim - 1)
    return jnp.where((lane % 64) < 32, up, dn)


def _inproj_kernel(x_ref, sc_ref, sh_ref, g_ref, w_ref, tqc_ref, tqs_ref, tkc_ref, tks_ref,
                   o_ref, h_ref, *, nq, nk, qb_lo, qb_hi, heads_per_tile, qscale):
    j = pl.program_id(1)

    @pl.when(j == 0)
    def _():
        x = x_ref[...]
        r = lax.rsqrt(jnp.mean(x * x, axis=-1, keepdims=True) + RMS_EPS)
        h = (x * r * g_ref[...]) * (1.0 + sc_ref[...]) + sh_ref[...]
        h_ref[...] = h.astype(BF16)

    acc = jnp.dot(h_ref[...], w_ref[...], preferred_element_type=F32)

    def normed_rope(tc_ref, ts_ref):
        tc = tc_ref[...]
        ts = ts_ref[...]
        for hh in range(heads_per_tile):
            xh = acc[:, hh * HEAD_DIM:(hh + 1) * HEAD_DIM]
            r = lax.rsqrt(jnp.mean(xh * xh, axis=-1, keepdims=True) + RMS_EPS)
            y = r * (xh * tc + _swap_pairs(xh) * ts)
            o_ref[:, hh * HEAD_DIM:(hh + 1) * HEAD_DIM] = y.astype(o_ref.dtype)

    @pl.when(j < nq)
    def _():
        normed_rope(tqc_ref, tqs_ref)

    @pl.when(jnp.logical_and(j >= nq, j < nq + nk))
    def _():
        normed_rope(tkc_ref, tks_ref)

    @pl.when(jnp.logical_and(j >= qb_lo, j < qb_hi))
    def _():
        o_ref[...] = (acc * qscale).astype(o_ref.dtype)

    @pl.when(jnp.logical_or(jnp.logical_and(j >= nq + nk, j < qb_lo), j >= qb_hi))
    def _():
        o_ref[...] = acc.astype(o_ref.dtype)


def _inproj(x2d, mod4, g_pre, w_in, tables, *, seq0, seq_len):
    n, d = x2d.shape
    cols = w_in.shape[1]
    a_width = A_HEADS * HEAD_DIM
    kv_width = A_KV_HEADS * HEAD_DIM
    b_width = B_HEADS * 2 * HEAD_DIM
    bm = min(INPROJ_BM, seq_len)
    bn = min(512, kv_width)
    assert seq_len % bm == 0 and a_width % bn == 0 and kv_width % bn == 0 and b_width % bn == 0
    nq, nk = a_width // bn, kv_width // bn
    qb_lo = (a_width + 2 * kv_width) // bn
    qb_hi = qb_lo + b_width // bn
    blocks_per_seq = seq_len // bm
    seq_of = lambda i: seq0 + i // blocks_per_seq
    tab_spec = pl.BlockSpec((bm, HEAD_DIM), lambda i, j: (i % blocks_per_seq, 0))
    kernel = functools.partial(_inproj_kernel, nq=nq, nk=nk, qb_lo=qb_lo, qb_hi=qb_hi,
                               heads_per_tile=bn // HEAD_DIM, qscale=HEAD_DIM ** -0.5)
    return pl.pallas_call(
        kernel,
        grid=(n // bm, cols // bn),
        in_specs=[pl.BlockSpec((bm, d), lambda i, j: (i, 0)),
                  pl.BlockSpec((None, None, 1, d), lambda i, j: (seq_of(i), 1, 0, 0)),
                  pl.BlockSpec((None, None, 1, d), lambda i, j: (seq_of(i), 0, 0, 0)),
                  pl.BlockSpec((1, d), lambda i, j: (0, 0)),
                  pl.BlockSpec((d, bn), lambda i, j: (0, j)),
                  tab_spec, tab_spec, tab_spec, tab_spec],
        out_specs=pl.BlockSpec((bm, bn), lambda i, j: (i, j)),
        out_shape=jax.ShapeDtypeStruct((n, cols), BF16),
        scratch_shapes=[pltpu.VMEM((bm, d), BF16)],
        compiler_params=_cparams(("parallel", "arbitrary")),
        name="inproj",
    )(x2d, mod4, mod4, g_pre, w_in, *tables)


def _rope_tables(seq_len, gain, scale):
    half = HEAD_DIM // 2
    inv = ROPE_THETA ** (-jnp.arange(0, half, 2, dtype=F32) / half)
    t = jnp.arange(seq_len, dtype=I32)
    row = (t // GRID_W).astype(F32)
    col = (t % GRID_W).astype(F32)
    ar = row[:, None] * inv[None, :]
    ac = col[:, None] * inv[None, :]
    cos = jnp.concatenate([jnp.cos(ar), jnp.cos(ar), jnp.cos(ac), jnp.cos(ac)], axis=-1)
    sin = jnp.concatenate([-jnp.sin(ar), jnp.sin(ar), -jnp.sin(ac), jnp.sin(ac)], axis=-1)
    g = gain.astype(F32).reshape(HEAD_DIM)
    lane = jnp.arange(HEAD_DIM)
    partner = jnp.where((lane % 64) < 32, lane + 32, lane - 32)
    return cos * (g * scale)[None, :], sin * (g[partner] * scale)[None, :]


def _online_softmax_step(s, v, m_ref, l_ref, acc_ref):
    m_prev = m_ref[...]
    m_new = jnp.maximum(m_prev, jnp.max(s, axis=-1, keepdims=True))
    alpha = jnp.exp(m_prev - m_new)
    p = jnp.exp(s - m_new)
    l_ref[...] = alpha * l_ref[...] + jnp.sum(p, axis=-1, keepdims=True)
    acc_ref[...] = alpha * acc_ref[...] + jnp.dot(p.astype(BF16), v, preferred_element_type=F32)
    m_ref[...] = m_new


_NT = (((1,), (1,)), ((), ()))


def _attn_a_kernel(q_ref, k_ref, v_ref, o_ref, q4_ref, m_ref, l_ref, acc_ref, *, group, bq, bk, nkv):
    for h in range(group):
        q4_ref[h * bq:(h + 1) * bq, :] = q_ref[:, h * HEAD_DIM:(h + 1) * HEAD_DIM]
    m_ref[...] = jnp.full(m_ref.shape, -jnp.inf, F32)
    l_ref[...] = jnp.zeros(l_ref.shape, F32)
    acc_ref[...] = jnp.zeros(acc_ref.shape, F32)

    def body(j, carry):
        off = pl.multiple_of(j * bk, bk)
        k = k_ref[pl.ds(off, bk), :]
        v = v_ref[pl.ds(off, bk), :]
        s = lax.dot_general(q4_ref[...], k, _NT, preferred_element_type=F32)
        _online_softmax_step(s, v, m_ref, l_ref, acc_ref)
        return carry

    lax.fori_loop(0, nkv, body, 0)
    o = acc_ref[...] / l_ref[...]
    for h in range(group):
        o_ref[:, h * HEAD_DIM:(h + 1) * HEAD_DIM] = o[h * bq:(h + 1) * bq, :].astype(o_ref.dtype)


def _attn_a(proj, batch, seq_len):
    group = A_HEADS // A_KV_HEADS
    a_width = A_HEADS * HEAD_DIM
    kv_width = A_KV_HEADS * HEAD_DIM
    bq = min(ATTN_A_BQ, seq_len)
    bk = min(ATTN_A_BK, seq_len)
    qblocks = seq_len // bq
    gw = group * HEAD_DIM
    k_col0 = a_width // HEAD_DIM
    v_col0 = (a_width + kv_width) // HEAD_DIM
    rows = group * bq
    kernel = functools.partial(_attn_a_kernel, group=group, bq=bq, bk=bk, nkv=seq_len // bk)
    return pl.pallas_call(
        kernel,
        grid=(batch, A_KV_HEADS, qblocks),
        in_specs=[pl.BlockSpec((bq, gw), lambda b, g, i: (b * qblocks + i, g)),
                  pl.BlockSpec((seq_len, HEAD_DIM), lambda b, g, i: (b, k_col0 + g),
                               pipeline_mode=pl.Buffered(1)),
                  pl.BlockSpec((seq_len, HEAD_DIM), lambda b, g, i: (b, v_col0 + g),
                               pipeline_mode=pl.Buffered(1))],
        out_specs=pl.BlockSpec((bq, gw), lambda b, g, i: (b * qblocks + i, g)),
        out_shape=jax.ShapeDtypeStruct((batch * seq_len, a_width), BF16),
        scratch_shapes=[pltpu.VMEM((rows, HEAD_DIM), BF16),
                        pltpu.VMEM((rows, 1), F32),
                        pltpu.VMEM((rows, 1), F32),
                        pltpu.VMEM((rows, HEAD_DIM), F32)],
        compiler_params=_cparams(("parallel", "parallel", "parallel")),
        name="attn_a",
    )(proj, proj, proj)


def _attn_b_kernel(slope_ref, lam_ref, q_ref, k_ref, v_ref, subln_ref, o_ref, m_ref, l_ref, acc_ref,
                   *, bq, bk, nkv):
    h = pl.program_id(1)
    i = pl.program_id(2)
    slope = slope_ref[h]
    lam = lam_ref[0]
    q = q_ref[...]
    q1 = q[:, :HEAD_DIM]
    q2 = q[:, HEAD_DIM:]
    qpos = (i * bq + lax.broadcasted_iota(I32, (bq, 1), 0)).astype(F32)
    m_ref[...] = jnp.full(m_ref.shape, -jnp.inf, F32)
    l_ref[...] = jnp.zeros(l_ref.shape, F32)
    acc_ref[...] = jnp.zeros(acc_ref.shape, F32)

    def body(j, carry):
        off = pl.multiple_of(j * bk, bk)
        k = k_ref[pl.ds(off, bk), :]
        v = v_ref[pl.ds(off, bk), :]
        kpos = (j * bk + lax.broadcasted_iota(I32, (1, bk), 1)).astype(F32)
        bias = -slope * jnp.abs(qpos - kpos)
        s1 = lax.dot_general(q1, k[:, :HEAD_DIM], _NT, preferred_element_type=F32) + bias
        s2 = lax.dot_general(q2, k[:, HEAD_DIM:], _NT, preferred_element_type=F32) + bias
        _online_softmax_step(jnp.concatenate([s1, s2], axis=0), v, m_ref, l_ref, acc_ref)
        return carry

    lax.fori_loop(0, nkv, body, 0)
    o = acc_ref[...] / l_ref[...]
    o = o[:bq, :] - lam * o[bq:, :]
    r = lax.rsqrt(jnp.mean(o * o, axis=-1, keepdims=True) + RMS_EPS)
    o_ref[...] = ((o * r * subln_ref[...]) * (1.0 - LAM_INIT)).astype(o_ref.dtype)


def _attn_b(proj, slopes, lam, subln, batch, seq_len):
    a_width = A_HEADS * HEAD_DIM
    kv_width = A_KV_HEADS * HEAD_DIM
    b_width = B_HEADS * 2 * HEAD_DIM
    hw = 2 * HEAD_DIM
    bq = min(ATTN_B_BQ, seq_len)
    bk = min(ATTN_B_BK, seq_len)
    qblocks = seq_len // bq
    base = a_width + 2 * kv_width
    assert base % hw == 0
    q_col0, k_col0, v_col0 = base // hw, (base + b_width) // hw, (base + 2 * b_width) // hw
    kernel = functools.partial(_attn_b_kernel, bq=bq, bk=bk, nkv=seq_len // bk)
    grid_spec = pltpu.PrefetchScalarGridSpec(
        num_scalar_prefetch=2,
        grid=(batch, B_HEADS, qblocks),
        in_specs=[pl.BlockSpec((bq, hw), lambda b, h, i, *_: (b * qblocks + i, q_col0 + h)),
                  pl.BlockSpec((seq_len, hw), lambda b, h, i, *_: (b, k_col0 + h),
                               pipeline_mode=pl.Buffered(1)),
                  pl.BlockSpec((seq_len, hw), lambda b, h, i, *_: (b, v_col0 + h),
                               pipeline_mode=pl.Buffered(1)),
                  pl.BlockSpec((1, hw), lambda b, h, i, *_: (0, 0))],
        out_specs=pl.BlockSpec((bq, hw), lambda b, h, i, *_: (b * qblocks + i, h)),
        scratch_shapes=[pltpu.VMEM((2 * bq, 1), F32),
                        pltpu.VMEM((2 * bq, 1), F32),
                        pltpu.VMEM((2 * bq, hw), F32)],
    )
    return pl.pallas_call(
        kernel,
        grid_spec=grid_spec,
        out_shape=jax.ShapeDtypeStruct((batch * seq_len, b_width), BF16),
        compiler_params=_cparams(("parallel", "parallel", "parallel")),
        name="attn_b",
    )(slopes, lam, proj, proj, proj, subln)


def _outmm_kernel(a1_ref, a2_ref, w1_ref, w2_ref, o_ref):
    o_ref[...] = (jnp.dot(a1_ref[...], w1_ref[...], preferred_element_type=F32)
                  + jnp.dot(a2_ref[...], w2_ref[...], preferred_element_type=F32))


def _outmm(oa, ob, w_out):
    n, ka = oa.shape
    kb = ob.shape[1]
    d = w_out.shape[1]
    bm = min(OUTMM_BM, n)
    bn = min(OUTMM_BN, d)
    assert ka == kb and ka % 16 == 0
    return pl.pallas_call(
        _outmm_kernel,
        grid=(n // bm, d // bn),
        in_specs=[pl.BlockSpec((bm, ka), lambda i, j: (i, 0)),
                  pl.BlockSpec((bm, kb), lambda i, j: (i, 0)),
                  pl.BlockSpec((ka, bn), lambda i, j: (0, j)),
                  pl.BlockSpec((kb, bn), lambda i, j: (1, j))],
        out_specs=pl.BlockSpec((bm, bn), lambda i, j: (i, j)),
        out_shape=jax.ShapeDtypeStruct((n, d), F32),
        compiler_params=_cparams(("parallel", "parallel")),
        name="outmm",
    )(oa, ob, w_out, w_out)


def _post_body(mix_ref, x_ref, gt_ref, sc_ref, sh_ref, gpost_ref, gpre_ref, wr_ref, br_ref,
               x1_ref, h2_ref, lg_ref):
    mix = mix_ref[...]
    r = lax.rsqrt(jnp.mean(mix * mix, axis=-1, keepdims=True) + RMS_EPS)
    x1 = x_ref[...] + gt_ref[...] * (mix * r * gpost_ref[...])
    x1_ref[...] = x1
    r2 = lax.rsqrt(jnp.mean(x1 * x1, axis=-1, keepdims=True) + RMS_EPS)
    h = (x1 * r2 * gpre_ref[...]) * (1.0 + sc_ref[...]) + sh_ref[...]
    hb = h.astype(BF16)
    half = h.shape[1] // 2
    hr = hb.astype(F32)
    lo = lax.bitcast_convert_type(hr[:, :half], U32) >> 16
    hi = lax.bitcast_convert_type(hr[:, half:], U32) & jnp.uint32(0xFFFF0000)
    h2_ref[...] = hi | lo
    lg_ref[...] = jnp.dot(hb, wr_ref[...], preferred_element_type=F32) + br_ref[...]


def _post_kernel(mixp_ref, mixs_ref, xp_ref, xs_ref, gt_ref, sc_ref, sh_ref, gpost_ref, gpre_ref,
                 wr_ref, br_ref, x1_ref, h2_ref, lg_ref, *, nbp):
    i = pl.program_id(0)
    rest = (gt_ref, sc_ref, sh_ref, gpost_ref, gpre_ref, wr_ref, br_ref, x1_ref, h2_ref, lg_ref)

    @pl.when(i < nbp)
    def _():
        _post_body(mixp_ref, xp_ref, *rest)

    @pl.when(i >= nbp)
    def _():
        _post_body(mixs_ref, xs_ref, *rest)


def _post(mix_p, mix_s, x_p, x_s, mod4, g_post, g_pre, w_router, b_router, *, seq_s, nseq_p):
    n_p, d = x_p.shape
    n_s = x_s.shape[0]
    bm = min(POST_BM, seq_s)
    assert n_p % bm == 0 and n_s % bm == 0 and seq_s % bm == 0
    nbp, nbs = n_p // bm, n_s // bm
    bps = seq_s // bm
    nt = n_p + n_s
    seq_of = lambda i: jnp.where(i < nbp, 0, nseq_p + (i - nbp) // bps)
    p_idx = lambda i: (jnp.minimum(i, nbp - 1), 0)
    s_idx = lambda i: (jnp.maximum(i - nbp, 0), 0)
    mod_spec = lambda which: pl.BlockSpec((None, None, 1, d), lambda i: (seq_of(i), which, 0, 0))
    row_spec = lambda c: pl.BlockSpec((bm, c), lambda i: (i, 0))
    return pl.pallas_call(
        functools.partial(_post_kernel, nbp=nbp),
        grid=(nbp + nbs,),
        in_specs=[pl.BlockSpec((bm, d), p_idx), pl.BlockSpec((bm, d), s_idx),
                  pl.BlockSpec((bm, d), p_idx), pl.BlockSpec((bm, d), s_idx),
                  mod_spec(2), mod_spec(4), mod_spec(3),
                  pl.BlockSpec((1, d), lambda i: (0, 0)),
                  pl.BlockSpec((1, d), lambda i: (0, 0)),
                  pl.BlockSpec((d, LANES), lambda i: (0, 0)),
                  pl.BlockSpec((1, LANES), lambda i: (0, 0))],
        out_specs=[row_spec(d), row_spec(d // 2), row_spec(LANES)],
        out_shape=[jax.ShapeDtypeStruct((nt, d), F32),
                   jax.ShapeDtypeStruct((nt, d // 2), U32),
                   jax.ShapeDtypeStruct((nt, LANES), F32)],
        compiler_params=_cparams(("parallel",)),
        name="post",
    )(mix_p, mix_s, x_p, x_s, mod4, mod4, mod4, g_post, g_pre, w_router, b_router)


def _route_kernel(lg_ref, idx_ref, w_ref, rank_ref, cnt_ref, carry_ref):
    i = pl.program_id(0)

    @pl.when(i == 0)
    def _():
        carry_ref[...] = jnp.zeros(carry_ref.shape, F32)

    lg = lg_ref[...]
    bt = lg.shape[0]
    lane = lax.broadcasted_iota(I32, lg.shape, 1)
    work = lg
    vals, idxs = [], []
    onehot = jnp.zeros(lg.shape, F32)
    for _ in range(TOP_K):
        mx = jnp.max(work, axis=-1, keepdims=True)
        ix = jnp.min(jnp.where(work == mx, lane, LANES), axis=-1, keepdims=True)
        sel = lane == ix
        vals.append(mx)
        idxs.append(ix)
        work = jnp.where(sel, -jnp.inf, work)
        onehot = onehot + sel.astype(F32)
    exps = [jnp.exp(v - vals[0]) for v in vals]
    denom = exps[0]
    for e in exps[1:]:
        denom = denom + e
    tri = (lax.broadcasted_iota(I32, (bt, bt), 0) > lax.broadcasted_iota(I32, (bt, bt), 1)).astype(BF16)
    prefix = jnp.dot(tri, onehot.astype(BF16), preferred_element_type=F32) + carry_ref[...]
    idx_out = jnp.zeros(lg.shape, I32)
    w_out = jnp.zeros(lg.shape, F32)
    rank_out = jnp.zeros(lg.shape, I32)
    for k in range(TOP_K):
        rk = jnp.sum(jnp.where(lane == idxs[k], prefix, 0.0), axis=-1, keepdims=True)
        idx_out = jnp.where(lane == k, idxs[k], idx_out)
        w_out = jnp.where(lane == k, exps[k] / denom, w_out)
        rank_out = jnp.where(lane == k, rk.astype(I32), rank_out)
    idx_ref[...] = idx_out
    w_ref[...] = w_out
    rank_ref[...] = rank_out
    carry_ref[...] = carry_ref[...] + jnp.sum(onehot, axis=0, keepdims=True)
    cnt_ref[...] = carry_ref[...]


def _route(logits):
    nt = logits.shape[0]
    bt = min(ROUTE_BT, nt)
    spec = pl.BlockSpec((bt, LANES), lambda i: (i, 0))
    return pl.pallas_call(
        _route_kernel,
        grid=(nt // bt,),
        in_specs=[spec],
        out_specs=[spec, spec, spec, pl.BlockSpec((1, LANES), lambda i: (0, 0))],
        out_shape=[jax.ShapeDtypeStruct((nt, LANES), I32),
                   jax.ShapeDtypeStruct((nt, LANES), F32),
                   jax.ShapeDtypeStruct((nt, LANES), I32),
                   jax.ShapeDtypeStruct((1, LANES), F32)],
        scratch_shapes=[pltpu.VMEM((1, LANES), F32)],
        compiler_params=_cparams(("arbitrary",)),
        name="route",
    )(logits)


def _dispatch_kernel(pad_ref, dest_ref, h_ref, xs_ref, zero_ref, sem, *, bt, n_experts, total):
    i = pl.program_id(0)

    def row_copy(src, r, d):
        return pltpu.make_async_copy(src.at[pl.ds(r, 1), :], xs_ref.at[pl.ds(d, 1), :], sem)

    @pl.when(i == 0)
    def _():
        zero_ref[...] = jnp.zeros(zero_ref.shape, zero_ref.dtype)

        def fill(lo, hi):
            def start(r, c):
                row_copy(zero_ref, 0, r).start()
                return c

            def wait(r, c):
                row_copy(zero_ref, 0, r).wait()
                return c

            lax.fori_loop(lo, hi, start, 0)
            lax.fori_loop(lo, hi, wait, 0)

        def per_expert(e, c):
            fill(pad_ref[0, e], pad_ref[1, e])
            return c

        lax.fori_loop(0, n_experts, per_expert, 0)
        fill(pad_ref[1, n_experts - 1], total)

    def start(t, c):
        for k in range(TOP_K):
            row_copy(h_ref, t, dest_ref[0, 0, t * TOP_K + k]).start()
        return c

    def wait(t, c):
        for k in range(TOP_K):
            row_copy(h_ref, t, dest_ref[0, 0, t * TOP_K + k]).wait()
        return c

    lax.fori_loop(0, bt, start, 0)
    lax.fori_loop(0, bt, wait, 0)


def _dispatch(pad_info, dest, h2p, total):
    nt, half = h2p.shape
    bt = min(DISPATCH_BT, nt)
    dest3 = dest.reshape(nt // bt, 1, bt * TOP_K)
    kernel = functools.partial(_dispatch_kernel, bt=bt, n_experts=N_EXPERTS, total=total)
    grid_spec = pltpu.PrefetchScalarGridSpec(
        num_scalar_prefetch=1,
        grid=(nt // bt,),
        in_specs=[pl.BlockSpec((1, 1, bt * TOP_K), lambda i, *_: (i, 0, 0), memory_space=pltpu.SMEM),
                  pl.BlockSpec((bt, half), lambda i, *_: (i, 0))],
        out_specs=pl.BlockSpec(memory_space=pl.ANY),
        scratch_shapes=[pltpu.VMEM((8, half), U32), pltpu.SemaphoreType.DMA(())],
    )
    return pl.pallas_call(
        kernel,
        grid_spec=grid_spec,
        out_shape=jax.ShapeDtypeStruct((total, half), U32),
        compiler_params=_cparams(("arbitrary",)),
        name="dispatch",
    )(pad_info, dest3, h2p)


def _unpack_rows(p):
    lo = lax.bitcast_convert_type(p << 16, F32)
    hi = lax.bitcast_convert_type(p & jnp.uint32(0xFFFF0000), F32)
    return lo.astype(BF16), hi.astype(BF16)


def _gu_kernel(be_ref, xs_ref, wg_ref, wu_ref, bg_ref, bu_ref, o_ref):
    lo, hi = _unpack_rows(xs_ref[...])
    half = lo.shape[1]

    def proj(w_ref, b_ref):
        return (jnp.dot(lo, w_ref[:half, :], preferred_element_type=F32)
                + jnp.dot(hi, w_ref[half:, :], preferred_element_type=F32) + b_ref[...])

    g = jnp.minimum(proj(wg_ref, bg_ref), SWIGLU_LIMIT)
    u = jnp.clip(proj(wu_ref, bu_ref), -SWIGLU_LIMIT, SWIGLU_LIMIT)
    o_ref[...] = ((u + 1.0) * (g * jax.nn.sigmoid(SWIGLU_ALPHA * g))).astype(o_ref.dtype)


def _gu(block_e, xs, w_gu, b_gu):
    total, half = xs.shape
    d = 2 * half
    n_e, _, two_f = w_gu.shape
    f = two_f // 2
    bm = MOE_BM
    tf = min(GU_TF, f)
    nft = f // tf
    grid_spec = pltpu.PrefetchScalarGridSpec(
        num_scalar_prefetch=1,
        grid=(nft, total // bm),
        in_specs=[pl.BlockSpec((bm, half), lambda j, i, be: (i, 0)),
                  pl.BlockSpec((None, d, tf), lambda j, i, be: (be[i], 0, j)),
                  pl.BlockSpec((None, d, tf), lambda j, i, be: (be[i], 0, nft + j)),
                  pl.BlockSpec((None, 1, tf), lambda j, i, be: (be[i], 0, j)),
                  pl.BlockSpec((None, 1, tf), lambda j, i, be: (be[i], 0, nft + j))],
        out_specs=pl.BlockSpec((bm, tf), lambda j, i, be: (i, j)),
    )
    return pl.pallas_call(
        _gu_kernel,
        grid_spec=grid_spec,
        out_shape=jax.ShapeDtypeStruct((total, f), BF16),
        compiler_params=_cparams(("parallel", "parallel")),
        name="moe_gu",
    )(block_e, xs, w_gu, w_gu, b_gu.reshape(n_e, 1, two_f), b_gu.reshape(n_e, 1, two_f))


def _down_kernel(be_ref, a_ref, w_ref, b_ref, o_ref):
    o_ref[...] = jnp.dot(a_ref[...], w_ref[...], preferred_element_type=F32) + b_ref[...]


def _down(block_e, act, w_down, b_down):
    total, f = act.shape
    n_e, _, d = w_down.shape
    bm = MOE_BM
    bn = min(DOWN_BN, d)
    grid_spec = pltpu.PrefetchScalarGridSpec(
        num_scalar_prefetch=1,
        grid=(d // bn, total // bm),
        in_specs=[pl.BlockSpec((bm, f), lambda j, i, be: (i, 0)),
                  pl.BlockSpec((None, f, bn), lambda j, i, be: (be[i], 0, j)),
                  pl.BlockSpec((None, 1, bn), lambda j, i, be: (be[i], 0, j))],
        out_specs=pl.BlockSpec((bm, bn), lambda j, i, be: (i, j)),
    )
    return pl.pallas_call(
        _down_kernel,
        grid_spec=grid_spec,
        out_shape=jax.ShapeDtypeStruct((total, d), F32),
        compiler_params=_cparams(("parallel", "parallel")),
        name="moe_down",
    )(block_e, act, w_down, b_down.reshape(n_e, 1, d))


def _final_kernel(dest_ref, w_ref, x1_ref, gt_ref, g_ref, eo_ref, y_ref, rows_ref, sem, *, bt):
    def row_copy(t, k):
        d = dest_ref[0, 0, t * TOP_K + k]
        return pltpu.make_async_copy(eo_ref.at[pl.ds(d, 1), :], rows_ref.at[k, pl.ds(t, 1), :], sem)

    def start(t, c):
        for k in range(TOP_K):
            row_copy(t, k).start()
        return c

    def wait(t, c):
        for k in range(TOP_K):
            row_copy(t, k).wait()
        return c

    lax.fori_loop(0, bt, start, 0)
    lax.fori_loop(0, bt, wait, 0)
    w = w_ref[...]
    f = w[:, 0:1] * rows_ref[0]
    for k in range(1, TOP_K):
        f = f + w[:, k:k + 1] * rows_ref[k]
    r = lax.rsqrt(jnp.mean(f * f, axis=-1, keepdims=True) + RMS_EPS)
    y_ref[...] = x1_ref[...] + gt_ref[...] * (f * r * g_ref[...])


def _final(dest, topw, x1_all, mod4, g_post, eo, *, row0, n, seq0, seq_len):
    d = x1_all.shape[1]
    bt = min(FINAL_BT, seq_len)
    assert row0 % bt == 0 and n % bt == 0 and seq_len % bt == 0
    b0 = row0 // bt
    bps = seq_len // bt
    nt = dest.shape[0]
    dest3 = dest.reshape(nt // bt, 1, bt * TOP_K)
    return pl.pallas_call(
        functools.partial(_final_kernel, bt=bt),
        grid=(n // bt,),
        in_specs=[pl.BlockSpec((1, 1, bt * TOP_K), lambda i: (b0 + i, 0, 0), memory_space=pltpu.SMEM),
                  pl.BlockSpec((bt, LANES), lambda i: (b0 + i, 0)),
                  pl.BlockSpec((bt, d), lambda i: (b0 + i, 0)),
                  pl.BlockSpec((None, None, 1, d), lambda i: (seq0 + i // bps, 5, 0, 0)),
                  pl.BlockSpec((1, d), lambda i: (0, 0)),
                  pl.BlockSpec(memory_space=pl.ANY)],
        out_specs=pl.BlockSpec((bt, d), lambda i: (i, 0)),
        out_shape=jax.ShapeDtypeStruct((n, d), F32),
        scratch_shapes=[pltpu.VMEM((TOP_K, bt, d), F32), pltpu.SemaphoreType.DMA(())],
        compiler_params=_cparams(("arbitrary",)),
        name="final",
    )(dest3, topw, x1_all, mod4, g_post, eo)


def kernel(x_prompt, x_sample, c_prompt, c_sample, w_ada, b_ada, g_pre_mix, g_post_mix, g_pre_ffn,
           g_post_ffn, w_in, w_out, q_norm, k_norm, lam_params, subln, w_router, b_router,
           w_gate_up, b_gate_up, w_down, b_down):
    bp, sp, d = x_prompt.shape
    bs, ss, _ = x_sample.shape
    assert bp == 1, "prompt group is a single sequence"
    n_p, n_s = bp * sp, bs * ss
    nt = n_p + n_s
    l = 0
    scale = HEAD_DIM ** -0.5

    w_in_b = w_in[l].astype(BF16)
    w_out_b = w_out[l].astype(BF16)
    w_gu_b = w_gate_up[l].astype(BF16)
    w_dn_b = w_down[l].astype(BF16)
    w_r_b = jnp.zeros((d, LANES), BF16).at[:, :N_EXPERTS].set(w_router[l].astype(BF16))
    b_r = jnp.full((1, LANES), NEG_BIG, F32).at[0, :N_EXPERTS].set(b_router[l].astype(F32))
    row = lambda g: g[l].astype(F32).reshape(1, -1)
    lp = lam_params[l].astype(F32)
    lam = (jnp.exp(jnp.sum(lp[0] * lp[1])) - jnp.exp(jnp.sum(lp[2] * lp[3])) + LAM_INIT).reshape(1)
    slopes = 2.0 ** (-8.0 * jnp.arange(1, B_HEADS + 1, dtype=F32) / B_HEADS)

    n_seq = bp + bs
    c_pad = jnp.zeros((16, d), F32).at[:n_seq].set(jnp.concatenate([c_prompt, c_sample], axis=0).astype(F32))
    mod = _ada(c_pad, w_ada[l], b_ada[l].astype(F32))
    mod4 = mod.reshape(16, N_MOD, 1, d)

    xp2 = x_prompt.reshape(n_p, d)
    xs2 = x_sample.reshape(n_s, d)
    groups = ((xp2, bp, sp, 0), (xs2, bs, ss, bp))
    mixes = []
    for x2, batch, seq_len, seq0 in groups:
        tables = (*_rope_tables(seq_len, q_norm[l], scale), *_rope_tables(seq_len, k_norm[l], 1.0))
        proj = _inproj(x2, mod4, row(g_pre_mix), w_in_b, tables, seq0=seq0, seq_len=seq_len)
        oa = _attn_a(proj, batch, seq_len)
        ob = _attn_b(proj, slopes, lam, row(subln), batch, seq_len)
        mixes.append(_outmm(oa, ob, w_out_b))

    x1_all, h2p, logits = _post(mixes[0], mixes[1], xp2, xs2, mod4, row(g_post_mix), row(g_pre_ffn),
                                w_r_b, b_r, seq_s=ss, nseq_p=bp)

    idx, topw, rank, cnt = _route(logits)
    counts = cnt[0, :N_EXPERTS].astype(I32)
    padded = ((counts + MOE_BM - 1) // MOE_BM) * MOE_BM
    ends = jnp.cumsum(padded)
    pad_start = ends - padded
    dest = (pad_start[idx[:, :TOP_K]] + rank[:, :TOP_K]).astype(I32)
    n_blocks = (nt * TOP_K) // MOE_BM + N_EXPERTS
    total = n_blocks * MOE_BM
    block_starts = jnp.arange(n_blocks, dtype=I32) * MOE_BM
    block_e = jnp.clip(jnp.searchsorted(ends, block_starts, side="right"), 0, N_EXPERTS - 1).astype(I32)
    pad_info = jnp.stack([pad_start + counts, ends]).astype(I32)

    xs = _dispatch(pad_info, dest, h2p, total)
    act = _gu(block_e, xs, w_gu_b, b_gate_up[l].astype(F32))
    eo = _down(block_e, act, w_dn_b, b_down[l].astype(F32))

    y_p = _final(dest, topw, x1_all, mod4, row(g_post_ffn), eo, row0=0, n=n_p, seq0=0, seq_len=sp)
    y_s = _final(dest, topw, x1_all, mod4, row(g_post_ffn), eo, row0=n_p, n=n_s, seq0=bp, seq_len=ss)
    return y_p.reshape(bp, sp, d), y_s.reshape(bs, ss, d)
```

```python
import functools
import math

import jax
import jax.numpy as jnp
from jax import lax
from jax.experimental import pallas as pl
from jax.experimental.pallas import tpu as pltpu

F32 = jnp.float32
BF16 = jnp.bfloat16
I32 = jnp.int32
U32 = jnp.uint32

HEAD_DIM = 128
GRID_W = 64
ROPE_THETA = 10000.0
RMS_EPS = 1e-6
A_HEADS = 16
A_KV_HEADS = 4
B_HEADS = 8
N_EXPERTS = 32
TOP_K = 4
SWIGLU_LIMIT = 7.0
SWIGLU_ALPHA = 1.702
N_MOD = 6
LAM_INIT = 0.8 - 0.6 * math.exp(-0.3 * 0)

LANES = 128
V7X_VMEM_BYTES = 64 * 1024 * 1024
VMEM_LIMIT = 56 * 1024 * 1024
NEG_BIG = -1e30

INPROJ_BM = 512
ATTN_A_BQ = 256
ATTN_A_BK = 512
ATTN_B_BQ = 512
ATTN_B_BK = 512
FAST_A_BQ = 256
FAST_A_BK = 2048
FAST_B_BQ = 512
FAST_B_BK = 512

SAFE_LOGIT = 40.0
EXP_ZERO_BELOW = -104.0
NORM_SLACK = 1.01
OUTMM_BM = 1024
OUTMM_BN = 512
POST_BM = 256
ROUTE_BT = 512
MOE_BM = 512
GU_TF = 512
DOWN_BN = 1024
DISPATCH_BT = 256
FINAL_BT = 128
ADA_BN = 512


def _cparams(sem):
    return pltpu.CompilerParams(dimension_semantics=("arbitrary",) * len(sem), vmem_limit_bytes=VMEM_LIMIT)


def _ada_kernel(c_ref, w_ref, b_ref, o_ref):
    c = c_ref[...]
    a = (c * jax.nn.sigmoid(c)).astype(BF16)
    o_ref[...] = jnp.dot(a, w_ref[...].astype(BF16), preferred_element_type=F32) + b_ref[...]


def _ada(c_pad, w_ada, b_ada):
    rows, d = c_pad.shape
    n = w_ada.shape[1]
    bn = min(ADA_BN, n)
    return pl.pallas_call(
        _ada_kernel,
        grid=(n // bn,),
        in_specs=[pl.BlockSpec((rows, d), lambda j: (0, 0)),
                  pl.BlockSpec((d, bn), lambda j: (0, j)),
                  pl.BlockSpec((1, bn), lambda j: (0, j))],
        out_specs=pl.BlockSpec((rows, bn), lambda j: (0, j)),
        out_shape=jax.ShapeDtypeStruct((rows, n), F32),
        compiler_params=_cparams(("parallel",)),
        name="ada",
    )(c_pad, w_ada, b_ada.reshape(1, n))


def _swap_pairs(x):
    n = x.shape[-1]
    lane = lax.broadcasted_iota(I32, x.shape, x.ndim - 1)
    up = pltpu.roll(x, n - 32, x.ndim - 1)
    dn = pltpu.roll(x, 32, x.ndim - 1)
    return jnp.where((lane % 64) < 32, up, dn)


def _inproj_kernel(x_ref, sc_ref, sh_ref, g_ref, w_ref, tqc_ref, tqs_ref, tkc_ref, tks_ref,
                   o_ref, n_ref, h_ref, *, nq, nk, qb_lo, qb_hi, kb_hi, heads_per_tile, qscale):
    j = pl.program_id(1)

    def max_sq_norm(y, best):
        n2 = jnp.max(jnp.sum(y * y, axis=-1, keepdims=True), axis=0, keepdims=True)
        return n2 if best is None else jnp.maximum(best, n2)

    @pl.when(j == 0)
    def _():
        x = x_ref[...]
        r = lax.rsqrt(jnp.mean(x * x, axis=-1, keepdims=True) + RMS_EPS)
        h = (x * r * g_ref[...]) * (1.0 + sc_ref[...]) + sh_ref[...]
        h_ref[...] = h.astype(BF16)

    acc = jnp.dot(h_ref[...], w_ref[...], preferred_element_type=F32)

    def put_norm(best):
        n_ref[...] = jnp.broadcast_to(best, n_ref.shape)

    def normed_rope(tc_ref, ts_ref):
        tc = tc_ref[...]
        ts = ts_ref[...]
        best = None
        for hh in range(heads_per_tile):
            xh = acc[:, hh * HEAD_DIM:(hh + 1) * HEAD_DIM]
            r = lax.rsqrt(jnp.mean(xh * xh, axis=-1, keepdims=True) + RMS_EPS)
            y = r * (xh * tc + _swap_pairs(xh) * ts)
            best = max_sq_norm(y, best)
            o_ref[:, hh * HEAD_DIM:(hh + 1) * HEAD_DIM] = y.astype(o_ref.dtype)
        put_norm(best)

    def scaled(factor):
        y = acc * factor
        best = None
        for hh in range(heads_per_tile):
            best = max_sq_norm(y[:, hh * HEAD_DIM:(hh + 1) * HEAD_DIM], best)
        o_ref[...] = y.astype(o_ref.dtype)
        put_norm(best)

    @pl.when(j < nq)
    def _():
        normed_rope(tqc_ref, tqs_ref)

    @pl.when(jnp.logical_and(j >= nq, j < nq + nk))
    def _():
        normed_rope(tkc_ref, tks_ref)

    @pl.when(jnp.logical_and(j >= qb_lo, j < qb_hi))
    def _():
        scaled(qscale)

    @pl.when(jnp.logical_and(j >= qb_hi, j < kb_hi))
    def _():
        scaled(1.0)

    @pl.when(jnp.logical_or(jnp.logical_and(j >= nq + nk, j < qb_lo), j >= kb_hi))
    def _():
        o_ref[...] = acc.astype(o_ref.dtype)
        n_ref[...] = jnp.zeros(n_ref.shape, F32)


def _inproj(x2d, mod4, g_pre, w_in, tables, *, seq0, seq_len):
    n, d = x2d.shape
    cols = w_in.shape[1]
    a_width = A_HEADS * HEAD_DIM
    kv_width = A_KV_HEADS * HEAD_DIM
    b_width = B_HEADS * 2 * HEAD_DIM
    bm = min(INPROJ_BM, seq_len)
    bn = min(512, kv_width)
    assert seq_len % bm == 0 and a_width % bn == 0 and kv_width % bn == 0 and b_width % bn == 0
    nq, nk = a_width // bn, kv_width // bn
    qb_lo = (a_width + 2 * kv_width) // bn
    qb_hi = qb_lo + b_width // bn
    kb_hi = qb_hi + b_width // bn
    blocks_per_seq = seq_len // bm
    seq_of = lambda i: seq0 + i // blocks_per_seq
    tab_spec = pl.BlockSpec((bm, HEAD_DIM), lambda i, j: (i % blocks_per_seq, 0))
    kernel = functools.partial(_inproj_kernel, nq=nq, nk=nk, qb_lo=qb_lo, qb_hi=qb_hi, kb_hi=kb_hi,
                               heads_per_tile=bn // HEAD_DIM, qscale=HEAD_DIM ** -0.5)
    proj, nrm = pl.pallas_call(
        kernel,
        grid=(n // bm, cols // bn),
        in_specs=[pl.BlockSpec((bm, d), lambda i, j: (i, 0)),
                  pl.BlockSpec((None, None, 1, d), lambda i, j: (seq_of(i), 1, 0, 0)),
                  pl.BlockSpec((None, None, 1, d), lambda i, j: (seq_of(i), 0, 0, 0)),
                  pl.BlockSpec((1, d), lambda i, j: (0, 0)),
                  pl.BlockSpec((d, bn), lambda i, j: (0, j)),
                  tab_spec, tab_spec, tab_spec, tab_spec],
        out_specs=[pl.BlockSpec((bm, bn), lambda i, j: (i, j)),
                   pl.BlockSpec((None, None, 8, LANES), lambda i, j: (i, j, 0, 0))],
        out_shape=[jax.ShapeDtypeStruct((n, cols), BF16),
                   jax.ShapeDtypeStruct((n // bm, cols // bn, 8, LANES), F32)],
        scratch_shapes=[pltpu.VMEM((bm, d), BF16)],
        compiler_params=_cparams(("parallel", "arbitrary")),
        name="inproj",
    )(x2d, mod4, mod4, g_pre, w_in, *tables)
    t = jnp.max(nrm[:, :, 0, 0], axis=0)
    bound = lambda qs, ks: jnp.sqrt(jnp.max(t[qs]) * jnp.max(t[ks])) * NORM_SLACK
    c_a = bound(slice(0, nq), slice(nq, nq + nk))
    c_b = bound(slice(qb_lo, qb_hi), slice(qb_hi, kb_hi))
    return proj, c_a, c_b


def _rope_tables(seq_len, gain, scale):
    half = HEAD_DIM // 2
    inv = ROPE_THETA ** (-jnp.arange(0, half, 2, dtype=F32) / half)
    t = jnp.arange(seq_len, dtype=I32)
    row = (t // GRID_W).astype(F32)
    col = (t % GRID_W).astype(F32)
    ar = row[:, None] * inv[None, :]
    ac = col[:, None] * inv[None, :]
    cos = jnp.concatenate([jnp.cos(ar), jnp.cos(ar), jnp.cos(ac), jnp.cos(ac)], axis=-1)
    sin = jnp.concatenate([-jnp.sin(ar), jnp.sin(ar), -jnp.sin(ac), jnp.sin(ac)], axis=-1)
    g = gain.astype(F32).reshape(HEAD_DIM)
    lane = jnp.arange(HEAD_DIM)
    partner = jnp.where((lane % 64) < 32, lane + 32, lane - 32)
    return cos * (g * scale)[None, :], sin * (g[partner] * scale)[None, :]


def _online_softmax_step(s, v, m_ref, l_ref, acc_ref):
    m_prev = m_ref[...]
    m_new = jnp.maximum(m_prev, jnp.max(s, axis=-1, keepdims=True))
    alpha = jnp.exp(m_prev - m_new)
    p = jnp.exp(s - m_new)
    l_ref[...] = alpha * l_ref[...] + jnp.sum(p, axis=-1, keepdims=True)
    acc_ref[...] = alpha * acc_ref[...] + jnp.dot(p.astype(BF16), v, preferred_element_type=F32)
    m_ref[...] = m_new


_NT = (((1,), (1,)), ((), ()))


def _attn_a_kernel(q_ref, k_ref, v_ref, o_ref, q4_ref, m_ref, l_ref, acc_ref, *, group, bq, bk, nkv):
    for h in range(group):
        q4_ref[h * bq:(h + 1) * bq, :] = q_ref[:, h * HEAD_DIM:(h + 1) * HEAD_DIM]
    m_ref[...] = jnp.full(m_ref.shape, -jnp.inf, F32)
    l_ref[...] = jnp.zeros(l_ref.shape, F32)
    acc_ref[...] = jnp.zeros(acc_ref.shape, F32)

    def body(j, carry):
        off = pl.multiple_of(j * bk, bk)
        k = k_ref[pl.ds(off, bk), :]
        v = v_ref[pl.ds(off, bk), :]
        s = lax.dot_general(q4_ref[...], k, _NT, preferred_element_type=F32)
        _online_softmax_step(s, v, m_ref, l_ref, acc_ref)
        return carry

    lax.fori_loop(0, nkv, body, 0)
    o = acc_ref[...] / l_ref[...]
    for h in range(group):
        o_ref[:, h * HEAD_DIM:(h + 1) * HEAD_DIM] = o[h * bq:(h + 1) * bq, :].astype(o_ref.dtype)


def _attn_a(proj, batch, seq_len):
    group = A_HEADS // A_KV_HEADS
    a_width = A_HEADS * HEAD_DIM
    kv_width = A_KV_HEADS * HEAD_DIM
    bq = min(ATTN_A_BQ, seq_len)
    bk = min(ATTN_A_BK, seq_len)
    qblocks = seq_len // bq
    gw = group * HEAD_DIM
    k_col0 = a_width // HEAD_DIM
    v_col0 = (a_width + kv_width) // HEAD_DIM
    rows = group * bq
    kernel = functools.partial(_attn_a_kernel, group=group, bq=bq, bk=bk, nkv=seq_len // bk)
    return pl.pallas_call(
        kernel,
        grid=(batch, A_KV_HEADS, qblocks),
        in_specs=[pl.BlockSpec((bq, gw), lambda b, g, i: (b * qblocks + i, g)),
                  pl.BlockSpec((seq_len, HEAD_DIM), lambda b, g, i: (b, k_col0 + g),
                               pipeline_mode=pl.Buffered(1)),
                  pl.BlockSpec((seq_len, HEAD_DIM), lambda b, g, i: (b, v_col0 + g),
                               pipeline_mode=pl.Buffered(1))],
        out_specs=pl.BlockSpec((bq, gw), lambda b, g, i: (b * qblocks + i, g)),
        out_shape=jax.ShapeDtypeStruct((batch * seq_len, a_width), BF16),
        scratch_shapes=[pltpu.VMEM((rows, HEAD_DIM), BF16),
                        pltpu.VMEM((rows, 1), F32),
                        pltpu.VMEM((rows, 1), F32),
                        pltpu.VMEM((rows, HEAD_DIM), F32)],
        compiler_params=_cparams(("parallel", "parallel", "parallel")),
        name="attn_a",
    )(proj, proj, proj)


def _attn_b_kernel(slope_ref, lam_ref, q_ref, k_ref, v_ref, subln_ref, o_ref, m_ref, l_ref, acc_ref,
                   *, bq, bk, nkv):
    h = pl.program_id(1)
    i = pl.program_id(2)
    slope = slope_ref[h]
    lam = lam_ref[0]
    q = q_ref[...]
    q1 = q[:, :HEAD_DIM]
    q2 = q[:, HEAD_DIM:]
    qpos = (i * bq + lax.broadcasted_iota(I32, (bq, 1), 0)).astype(F32)
    m_ref[...] = jnp.full(m_ref.shape, -jnp.inf, F32)
    l_ref[...] = jnp.zeros(l_ref.shape, F32)
    acc_ref[...] = jnp.zeros(acc_ref.shape, F32)

    def body(j, carry):
        off = pl.multiple_of(j * bk, bk)
        k = k_ref[pl.ds(off, bk), :]
        v = v_ref[pl.ds(off, bk), :]
        kpos = (j * bk + lax.broadcasted_iota(I32, (1, bk), 1)).astype(F32)
        bias = -slope * jnp.abs(qpos - kpos)
        s1 = lax.dot_general(q1, k[:, :HEAD_DIM], _NT, preferred_element_type=F32) + bias
        s2 = lax.dot_general(q2, k[:, HEAD_DIM:], _NT, preferred_element_type=F32) + bias
        _online_softmax_step(jnp.concatenate([s1, s2], axis=0), v, m_ref, l_ref, acc_ref)
        return carry

    lax.fori_loop(0, nkv, body, 0)
    o = acc_ref[...] / l_ref[...]
    o = o[:bq, :] - lam * o[bq:, :]
    r = lax.rsqrt(jnp.mean(o * o, axis=-1, keepdims=True) + RMS_EPS)
    o_ref[...] = ((o * r * subln_ref[...]) * (1.0 - LAM_INIT)).astype(o_ref.dtype)


def _attn_b(proj, slopes, lam, subln, batch, seq_len):
    a_width = A_HEADS * HEAD_DIM
    kv_width = A_KV_HEADS * HEAD_DIM
    b_width = B_HEADS * 2 * HEAD_DIM
    hw = 2 * HEAD_DIM
    bq = min(ATTN_B_BQ, seq_len)
    bk = min(ATTN_B_BK, seq_len)
    qblocks = seq_len // bq
    base = a_width + 2 * kv_width
    assert base % hw == 0
    q_col0, k_col0, v_col0 = base // hw, (base + b_width) // hw, (base + 2 * b_width) // hw
    kernel = functools.partial(_attn_b_kernel, bq=bq, bk=bk, nkv=seq_len // bk)
    grid_spec = pltpu.PrefetchScalarGridSpec(
        num_scalar_prefetch=2,
        grid=(batch, B_HEADS, qblocks),
        in_specs=[pl.BlockSpec((bq, hw), lambda b, h, i, *_: (b * qblocks + i, q_col0 + h)),
                  pl.BlockSpec((seq_len, hw), lambda b, h, i, *_: (b, k_col0 + h),
                               pipeline_mode=pl.Buffered(1)),
                  pl.BlockSpec((seq_len, hw), lambda b, h, i, *_: (b, v_col0 + h),
                               pipeline_mode=pl.Buffered(1)),
                  pl.BlockSpec((1, hw), lambda b, h, i, *_: (0, 0))],
        out_specs=pl.BlockSpec((bq, hw), lambda b, h, i, *_: (b * qblocks + i, h)),
        scratch_shapes=[pltpu.VMEM((2 * bq, 1), F32),
                        pltpu.VMEM((2 * bq, 1), F32),
                        pltpu.VMEM((2 * bq, hw), F32)],
    )
    return pl.pallas_call(
        kernel,
        grid_spec=grid_spec,
        out_shape=jax.ShapeDtypeStruct((batch * seq_len, b_width), BF16),
        compiler_params=_cparams(("parallel", "parallel", "parallel")),
        name="attn_b",
    )(slopes, lam, proj, proj, proj, subln)


def _lane_partial_sum(p):
    ps = p[:, 0:LANES]
    for t in range(1, p.shape[1] // LANES):
        ps = ps + p[:, t * LANES:(t + 1) * LANES]
    return ps


def _attn_a_fast_kernel(q_ref, k_ref, v_ref, o_ref, q4_ref, l_ref, acc_ref, *, group, bq, bk, nkv):
    for h in range(group):
        q4_ref[h * bq:(h + 1) * bq, :] = q_ref[:, h * HEAD_DIM:(h + 1) * HEAD_DIM]
    l_ref[...] = jnp.zeros(l_ref.shape, F32)
    acc_ref[...] = jnp.zeros(acc_ref.shape, F32)

    def body(j, carry):
        off = pl.multiple_of(j * bk, bk)
        k = k_ref[pl.ds(off, bk), :]
        v = v_ref[pl.ds(off, bk), :]
        p = jnp.exp(lax.dot_general(q4_ref[...], k, _NT, preferred_element_type=F32))
        l_ref[...] += _lane_partial_sum(p)
        acc_ref[...] += jnp.dot(p.astype(BF16), v, preferred_element_type=F32)
        return carry

    lax.fori_loop(0, nkv, body, 0)
    o = acc_ref[...] / jnp.sum(l_ref[...], axis=-1, keepdims=True)
    for h in range(group):
        o_ref[:, h * HEAD_DIM:(h + 1) * HEAD_DIM] = o[h * bq:(h + 1) * bq, :].astype(o_ref.dtype)


def _attn_a_fast(proj, batch, seq_len):
    group = A_HEADS // A_KV_HEADS
    a_width = A_HEADS * HEAD_DIM
    kv_width = A_KV_HEADS * HEAD_DIM
    bq = min(FAST_A_BQ, seq_len)
    bk = min(FAST_A_BK, seq_len)
    qblocks = seq_len // bq
    gw = group * HEAD_DIM
    k_col0 = a_width // HEAD_DIM
    v_col0 = (a_width + kv_width) // HEAD_DIM
    rows = group * bq
    kernel = functools.partial(_attn_a_fast_kernel, group=group, bq=bq, bk=bk, nkv=seq_len // bk)
    return pl.pallas_call(
        kernel,
        grid=(batch, A_KV_HEADS, qblocks),
        in_specs=[pl.BlockSpec((bq, gw), lambda b, g, i: (b * qblocks + i, g)),
                  pl.BlockSpec((seq_len, HEAD_DIM), lambda b, g, i: (b, k_col0 + g),
                               pipeline_mode=pl.Buffered(1)),
                  pl.BlockSpec((seq_len, HEAD_DIM), lambda b, g, i: (b, v_col0 + g),
                               pipeline_mode=pl.Buffered(1))],
        out_specs=pl.BlockSpec((bq, gw), lambda b, g, i: (b * qblocks + i, g)),
        out_shape=jax.ShapeDtypeStruct((batch * seq_len, a_width), BF16),
        scratch_shapes=[pltpu.VMEM((rows, HEAD_DIM), BF16),
                        pltpu.VMEM((rows, LANES), F32),
                        pltpu.VMEM((rows, HEAD_DIM), F32)],
        compiler_params=_cparams(("parallel", "parallel", "parallel")),
        name="attn_a_fast",
    )(proj, proj, proj)


def _attn_b_fast_kernel(slope_ref, lam_ref, reach_ref, q_ref, k_ref, v_ref, subln_ref, o_ref, l_ref, acc_ref,
                        *, bq, bk, nkv):
    h = pl.program_id(1)
    i = pl.program_id(2)
    slope = slope_ref[h]
    lam = lam_ref[0]
    reach = reach_ref[h]
    q = q_ref[...]
    q1 = q[:, :HEAD_DIM]
    q2 = q[:, HEAD_DIM:]
    i0 = i * bq
    qpos = (i0 + lax.broadcasted_iota(I32, (bq, 1), 0)).astype(F32)
    l_ref[...] = jnp.zeros(l_ref.shape, F32)
    acc_ref[...] = jnp.zeros(acc_ref.shape, F32)
    jb_lo = jnp.maximum(i0 - reach, 0) // bk
    jb_hi = jnp.minimum((i0 + bq - 1 + reach) // bk + 1, nkv)

    def body(j, carry):
        off = pl.multiple_of(j * bk, bk)
        k = k_ref[pl.ds(off, bk), :]
        v = v_ref[pl.ds(off, bk), :]
        kpos = (j * bk + lax.broadcasted_iota(I32, (1, bk), 1)).astype(F32)
        bias = -slope * jnp.abs(qpos - kpos)
        p1 = jnp.exp(lax.dot_general(q1, k[:, :HEAD_DIM], _NT, preferred_element_type=F32) + bias)
        p2 = jnp.exp(lax.dot_general(q2, k[:, HEAD_DIM:], _NT, preferred_element_type=F32) + bias)
        l_ref[:bq, :] += _lane_partial_sum(p1)
        l_ref[bq:, :] += _lane_partial_sum(p2)
        p = jnp.concatenate([p1, p2], axis=0).astype(BF16)
        acc_ref[...] += jnp.dot(p, v, preferred_element_type=F32)
        return carry

    lax.fori_loop(jb_lo, jb_hi, body, 0)
    o = acc_ref[...] / jnp.sum(l_ref[...], axis=-1, keepdims=True)
    o = o[:bq, :] - lam * o[bq:, :]
    r = lax.rsqrt(jnp.mean(o * o, axis=-1, keepdims=True) + RMS_EPS)
    o_ref[...] = ((o * r * subln_ref[...]) * (1.0 - LAM_INIT)).astype(o_ref.dtype)


def _attn_b_fast(proj, slopes, lam, reach, subln, batch, seq_len):
    a_width = A_HEADS * HEAD_DIM
    kv_width = A_KV_HEADS * HEAD_DIM
    b_width = B_HEADS * 2 * HEAD_DIM
    hw = 2 * HEAD_DIM
    bq = min(FAST_B_BQ, seq_len)
    bk = min(FAST_B_BK, seq_len)
    qblocks = seq_len // bq
    base = a_width + 2 * kv_width
    assert base % hw == 0
    q_col0, k_col0, v_col0 = base // hw, (base + b_width) // hw, (base + 2 * b_width) // hw
    kernel = functools.partial(_attn_b_fast_kernel, bq=bq, bk=bk, nkv=seq_len // bk)
    grid_spec = pltpu.PrefetchScalarGridSpec(
        num_scalar_prefetch=3,
        grid=(batch, B_HEADS, qblocks),
        in_specs=[pl.BlockSpec((bq, hw), lambda b, h, i, *_: (b * qblocks + i, q_col0 + h)),
                  pl.BlockSpec((seq_len, hw), lambda b, h, i, *_: (b, k_col0 + h),
                               pipeline_mode=pl.Buffered(1)),
                  pl.BlockSpec((seq_len, hw), lambda b, h, i, *_: (b, v_col0 + h),
                               pipeline_mode=pl.Buffered(1)),
                  pl.BlockSpec((1, hw), lambda b, h, i, *_: (0, 0))],
        out_specs=pl.BlockSpec((bq, hw), lambda b, h, i, *_: (b * qblocks + i, h)),
        scratch_shapes=[pltpu.VMEM((2 * bq, LANES), F32),
                        pltpu.VMEM((2 * bq, hw), F32)],
    )
    return pl.pallas_call(
        kernel,
        grid_spec=grid_spec,
        out_shape=jax.ShapeDtypeStruct((batch * seq_len, b_width), BF16),
        compiler_params=_cparams(("parallel", "parallel", "parallel")),
        name="attn_b_fast",
    )(slopes, lam, reach, proj, proj, proj, subln)


def _attention(proj, c_a, c_b, slopes, lam, subln, batch, seq_len):
    oa = lax.cond(c_a <= SAFE_LOGIT,
                  lambda p: _attn_a_fast(p, batch, seq_len),
                  lambda p: _attn_a(p, batch, seq_len), proj)
    reach = jnp.clip(jnp.ceil((c_b - EXP_ZERO_BELOW) / slopes), 0, seq_len).astype(I32)
    ob = lax.cond(c_b <= SAFE_LOGIT,
                  lambda p: _attn_b_fast(p, slopes, lam, reach, subln, batch, seq_len),
                  lambda p: _attn_b(p, slopes, lam, subln, batch, seq_len), proj)
    return oa, ob


def _outmm_kernel(a1_ref, a2_ref, w1_ref, w2_ref, o_ref):
    o_ref[...] = (jnp.dot(a1_ref[...], w1_ref[...], preferred_element_type=F32)
                  + jnp.dot(a2_ref[...], w2_ref[...], preferred_element_type=F32))


def _outmm(oa, ob, w_out):
    n, ka = oa.shape
    kb = ob.shape[1]
    d = w_out.shape[1]
    bm = min(OUTMM_BM, n)
    bn = min(OUTMM_BN, d)
    assert ka == kb and ka % 16 == 0
    return pl.pallas_call(
        _outmm_kernel,
        grid=(n // bm, d // bn),
        in_specs=[pl.BlockSpec((bm, ka), lambda i, j: (i, 0)),
                  pl.BlockSpec((bm, kb), lambda i, j: (i, 0)),
                  pl.BlockSpec((ka, bn), lambda i, j: (0, j)),
                  pl.BlockSpec((kb, bn), lambda i, j: (1, j))],
        out_specs=pl.BlockSpec((bm, bn), lambda i, j: (i, j)),
        out_shape=jax.ShapeDtypeStruct((n, d), F32),
        compiler_params=_cparams(("parallel", "parallel")),
        name="outmm",
    )(oa, ob, w_out, w_out)


def _post_body(mix_ref, x_ref, gt_ref, sc_ref, sh_ref, gpost_ref, gpre_ref, wr_ref, br_ref,
               x1_ref, h2_ref, lg_ref):
    mix = mix_ref[...]
    r = lax.rsqrt(jnp.mean(mix * mix, axis=-1, keepdims=True) + RMS_EPS)
    x1 = x_ref[...] + gt_ref[...] * (mix * r * gpost_ref[...])
    x1_ref[...] = x1
    r2 = lax.rsqrt(jnp.mean(x1 * x1, axis=-1, keepdims=True) + RMS_EPS)
    h = (x1 * r2 * gpre_ref[...]) * (1.0 + sc_ref[...]) + sh_ref[...]
    hb = h.astype(BF16)
    half = h.shape[1] // 2
    hr = hb.astype(F32)
    lo = lax.bitcast_convert_type(hr[:, :half], U32) >> 16
    hi = lax.bitcast_convert_type(hr[:, half:], U32) & jnp.uint32(0xFFFF0000)
    h2_ref[...] = hi | lo
    lg_ref[...] = jnp.dot(hb, wr_ref[...], preferred_element_type=F32) + br_ref[...]


def _post_kernel(mixp_ref, mixs_ref, xp_ref, xs_ref, gt_ref, sc_ref, sh_ref, gpost_ref, gpre_ref,
                 wr_ref, br_ref, x1_ref, h2_ref, lg_ref, *, nbp):
    i = pl.program_id(0)
    rest = (gt_ref, sc_ref, sh_ref, gpost_ref, gpre_ref, wr_ref, br_ref, x1_ref, h2_ref, lg_ref)

    @pl.when(i < nbp)
    def _():
        _post_body(mixp_ref, xp_ref, *rest)

    @pl.when(i >= nbp)
    def _():
        _post_body(mixs_ref, xs_ref, *rest)


def _post(mix_p, mix_s, x_p, x_s, mod4, g_post, g_pre, w_router, b_router, *, seq_s, nseq_p):
    n_p, d = x_p.shape
    n_s = x_s.shape[0]
    bm = min(POST_BM, seq_s)
    assert n_p % bm == 0 and n_s % bm == 0 and seq_s % bm == 0
    nbp, nbs = n_p // bm, n_s // bm
    bps = seq_s // bm
    nt = n_p + n_s
    seq_of = lambda i: jnp.where(i < nbp, 0, nseq_p + (i - nbp) // bps)
    p_idx = lambda i: (jnp.minimum(i, nbp - 1), 0)
    s_idx = lambda i: (jnp.maximum(i - nbp, 0), 0)
    mod_spec = lambda which: pl.BlockSpec((None, None, 1, d), lambda i: (seq_of(i), which, 0, 0))
    row_spec = lambda c: pl.BlockSpec((bm, c), lambda i: (i, 0))
    return pl.pallas_call(
        functools.partial(_post_kernel, nbp=nbp),
        grid=(nbp + nbs,),
        in_specs=[pl.BlockSpec((bm, d), p_idx), pl.BlockSpec((bm, d), s_idx),
                  pl.BlockSpec((bm, d), p_idx), pl.BlockSpec((bm, d), s_idx),
                  mod_spec(2), mod_spec(4), mod_spec(3),
                  pl.BlockSpec((1, d), lambda i: (0, 0)),
                  pl.BlockSpec((1, d), lambda i: (0, 0)),
                  pl.BlockSpec((d, LANES), lambda i: (0, 0)),
                  pl.BlockSpec((1, LANES), lambda i: (0, 0))],
        out_specs=[row_spec(d), row_spec(d // 2), row_spec(LANES)],
        out_shape=[jax.ShapeDtypeStruct((nt, d), F32),
                   jax.ShapeDtypeStruct((nt, d // 2), U32),
                   jax.ShapeDtypeStruct((nt, LANES), F32)],
        compiler_params=_cparams(("parallel",)),
        name="post",
    )(mix_p, mix_s, x_p, x_s, mod4, mod4, mod4, g_post, g_pre, w_router, b_router)


def _route_kernel(lg_ref, idx_ref, w_ref, rank_ref, cnt_ref, carry_ref):
    i = pl.program_id(0)

    @pl.when(i == 0)
    def _():
        carry_ref[...] = jnp.zeros(carry_ref.shape, F32)

    lg = lg_ref[...]
    bt = lg.shape[0]
    lane = lax.broadcasted_iota(I32, lg.shape, 1)
    work = lg
    vals, idxs = [], []
    onehot = jnp.zeros(lg.shape, F32)
    for _ in range(TOP_K):
        mx = jnp.max(work, axis=-1, keepdims=True)
        ix = jnp.min(jnp.where(work == mx, lane, LANES), axis=-1, keepdims=True)
        sel = lane == ix
        vals.append(mx)
        idxs.append(ix)
        work = jnp.where(sel, -jnp.inf, work)
        onehot = onehot + sel.astype(F32)
    exps = [jnp.exp(v - vals[0]) for v in vals]
    denom = exps[0]
    for e in exps[1:]:
        denom = denom + e
    tri = (lax.broadcasted_iota(I32, (bt, bt), 0) > lax.broadcasted_iota(I32, (bt, bt), 1)).astype(BF16)
    prefix = jnp.dot(tri, onehot.astype(BF16), preferred_element_type=F32) + carry_ref[...]
    idx_out = jnp.zeros(lg.shape, I32)
    w_out = jnp.zeros(lg.shape, F32)
    rank_out = jnp.zeros(lg.shape, I32)
    for k in range(TOP_K):
        rk = jnp.sum(jnp.where(lane == idxs[k], prefix, 0.0), axis=-1, keepdims=True)
        idx_out = jnp.where(lane == k, idxs[k], idx_out)
        w_out = jnp.where(lane == k, exps[k] / denom, w_out)
        rank_out = jnp.where(lane == k, rk.astype(I32), rank_out)
    idx_ref[...] = idx_out
    w_ref[...] = w_out
    rank_ref[...] = rank_out
    carry_ref[...] = carry_ref[...] + jnp.sum(onehot, axis=0, keepdims=True)
    cnt_ref[...] = carry_ref[...]


def _route(logits):
    nt = logits.shape[0]
    bt = min(ROUTE_BT, nt)
    spec = pl.BlockSpec((bt, LANES), lambda i: (i, 0))
    return pl.pallas_call(
        _route_kernel,
        grid=(nt // bt,),
        in_specs=[spec],
        out_specs=[spec, spec, spec, pl.BlockSpec((1, LANES), lambda i: (0, 0))],
        out_shape=[jax.ShapeDtypeStruct((nt, LANES), I32),
                   jax.ShapeDtypeStruct((nt, LANES), F32),
                   jax.ShapeDtypeStruct((nt, LANES), I32),
                   jax.ShapeDtypeStruct((1, LANES), F32)],
        scratch_shapes=[pltpu.VMEM((1, LANES), F32)],
        compiler_params=_cparams(("arbitrary",)),
        name="route",
    )(logits)


def _dispatch_kernel(pad_ref, dest_ref, h_ref, xs_ref, zero_ref, sem, *, bt, n_experts, total):
    i = pl.program_id(0)

    def row_copy(src, r, d):
        return pltpu.make_async_copy(src.at[pl.ds(r, 1), :], xs_ref.at[pl.ds(d, 1), :], sem)

    @pl.when(i == 0)
    def _():
        zero_ref[...] = jnp.zeros(zero_ref.shape, zero_ref.dtype)

        def fill(lo, hi):
            def start(r, c):
                row_copy(zero_ref, 0, r).start()
                return c

            def wait(r, c):
                row_copy(zero_ref, 0, r).wait()
                return c

            lax.fori_loop(lo, hi, start, 0)
            lax.fori_loop(lo, hi, wait, 0)

        def per_expert(e, c):
            fill(pad_ref[0, e], pad_ref[1, e])
            return c

        lax.fori_loop(0, n_experts, per_expert, 0)
        fill(pad_ref[1, n_experts - 1], total)

    def start(t, c):
        for k in range(TOP_K):
            row_copy(h_ref, t, dest_ref[0, 0, t * TOP_K + k]).start()
        return c

    def wait(t, c):
        for k in range(TOP_K):
            row_copy(h_ref, t, dest_ref[0, 0, t * TOP_K + k]).wait()
        return c

    lax.fori_loop(0, bt, start, 0)
    lax.fori_loop(0, bt, wait, 0)


def _dispatch(pad_info, dest, h2p, total):
    nt, half = h2p.shape
    bt = min(DISPATCH_BT, nt)
    dest3 = dest.reshape(nt // bt, 1, bt * TOP_K)
    kernel = functools.partial(_dispatch_kernel, bt=bt, n_experts=N_EXPERTS, total=total)
    grid_spec = pltpu.PrefetchScalarGridSpec(
        num_scalar_prefetch=1,
        grid=(nt // bt,),
        in_specs=[pl.BlockSpec((1, 1, bt * TOP_K), lambda i, *_: (i, 0, 0), memory_space=pltpu.SMEM),
                  pl.BlockSpec((bt, half), lambda i, *_: (i, 0))],
        out_specs=pl.BlockSpec(memory_space=pl.ANY),
        scratch_shapes=[pltpu.VMEM((8, half), U32), pltpu.SemaphoreType.DMA(())],
    )
    return pl.pallas_call(
        kernel,
        grid_spec=grid_spec,
        out_shape=jax.ShapeDtypeStruct((total, half), U32),
        compiler_params=_cparams(("arbitrary",)),
        name="dispatch",
    )(pad_info, dest3, h2p)


def _unpack_rows(p):
    lo = lax.bitcast_convert_type(p << 16, F32)
    hi = lax.bitcast_convert_type(p & jnp.uint32(0xFFFF0000), F32)
    return lo.astype(BF16), hi.astype(BF16)


def _gu_kernel(be_ref, bv_ref, xs_ref, wg_ref, wu_ref, bg_ref, bu_ref, o_ref):
    @pl.when(bv_ref[pl.program_id(1)] > 0)
    def _():
        lo, hi = _unpack_rows(xs_ref[...])
        half = lo.shape[1]

        def proj(w_ref, b_ref):
            return (jnp.dot(lo, w_ref[:half, :], preferred_element_type=F32)
                    + jnp.dot(hi, w_ref[half:, :], preferred_element_type=F32) + b_ref[...])

        g = jnp.minimum(proj(wg_ref, bg_ref), SWIGLU_LIMIT)
        u = jnp.clip(proj(wu_ref, bu_ref), -SWIGLU_LIMIT, SWIGLU_LIMIT)
        o_ref[...] = ((u + 1.0) * (g * jax.nn.sigmoid(SWIGLU_ALPHA * g))).astype(o_ref.dtype)


def _gu(block_e, block_valid, xs, w_gu, b_gu):
    total, half = xs.shape
    d = 2 * half
    n_e, _, two_f = w_gu.shape
    f = two_f // 2
    bm = MOE_BM
    tf = min(GU_TF, f)
    nft = f // tf
    grid_spec = pltpu.PrefetchScalarGridSpec(
        num_scalar_prefetch=2,
        grid=(nft, total // bm),
        in_specs=[pl.BlockSpec((bm, half), lambda j, i, be, bv: (i, 0)),
                  pl.BlockSpec((None, d, tf), lambda j, i, be, bv: (be[i], 0, j)),
                  pl.BlockSpec((None, d, tf), lambda j, i, be, bv: (be[i], 0, nft + j)),
                  pl.BlockSpec((None, 1, tf), lambda j, i, be, bv: (be[i], 0, j)),
                  pl.BlockSpec((None, 1, tf), lambda j, i, be, bv: (be[i], 0, nft + j))],
        out_specs=pl.BlockSpec((bm, tf), lambda j, i, be, bv: (i, j)),
    )
    return pl.pallas_call(
        _gu_kernel,
        grid_spec=grid_spec,
        out_shape=jax.ShapeDtypeStruct((total, f), BF16),
        compiler_params=_cparams(("parallel", "parallel")),
        name="moe_gu",
    )(block_e, block_valid, xs, w_gu, w_gu, b_gu.reshape(n_e, 1, two_f), b_gu.reshape(n_e, 1, two_f))


def _down_kernel(be_ref, bv_ref, a_ref, w_ref, b_ref, o_ref):
    @pl.when(bv_ref[pl.program_id(1)] > 0)
    def _():
        o_ref[...] = jnp.dot(a_ref[...], w_ref[...], preferred_element_type=F32) + b_ref[...]


def _down(block_e, block_valid, act, w_down, b_down):
    total, f = act.shape
    n_e, _, d = w_down.shape
    bm = MOE_BM
    bn = min(DOWN_BN, d)
    grid_spec = pltpu.PrefetchScalarGridSpec(
        num_scalar_prefetch=2,
        grid=(d // bn, total // bm),
        in_specs=[pl.BlockSpec((bm, f), lambda j, i, be, bv: (i, 0)),
                  pl.BlockSpec((None, f, bn), lambda j, i, be, bv: (be[i], 0, j)),
                  pl.BlockSpec((None, 1, bn), lambda j, i, be, bv: (be[i], 0, j))],
        out_specs=pl.BlockSpec((bm, bn), lambda j, i, be, bv: (i, j)),
    )
    return pl.pallas_call(
        _down_kernel,
        grid_spec=grid_spec,
        out_shape=jax.ShapeDtypeStruct((total, d), F32),
        compiler_params=_cparams(("parallel", "parallel")),
        name="moe_down",
    )(block_e, block_valid, act, w_down, b_down.reshape(n_e, 1, d))


def _final_kernel(dest_ref, w_ref, x1_ref, gt_ref, g_ref, eo_ref, y_ref, rows_ref, sem, *, bt):
    def row_copy(t, k):
        d = dest_ref[0, 0, t * TOP_K + k]
        return pltpu.make_async_copy(eo_ref.at[pl.ds(d, 1), :], rows_ref.at[k, pl.ds(t, 1), :], sem)

    def start(t, c):
        for k in range(TOP_K):
            row_copy(t, k).start()
        return c

    def wait(t, c):
        for k in range(TOP_K):
            row_copy(t, k).wait()
        return c

    lax.fori_loop(0, bt, start, 0)
    lax.fori_loop(0, bt, wait, 0)
    w = w_ref[...]
    f = w[:, 0:1] * rows_ref[0]
    for k in range(1, TOP_K):
        f = f + w[:, k:k + 1] * rows_ref[k]
    r = lax.rsqrt(jnp.mean(f * f, axis=-1, keepdims=True) + RMS_EPS)
    y_ref[...] = x1_ref[...] + gt_ref[...] * (f * r * g_ref[...])


def _final(dest, topw, x1_all, mod4, g_post, eo, *, row0, n, seq0, seq_len):
    d = x1_all.shape[1]
    bt = min(FINAL_BT, seq_len)
    assert row0 % bt == 0 and n % bt == 0 and seq_len % bt == 0
    b0 = row0 // bt
    bps = seq_len // bt
    nt = dest.shape[0]
    dest3 = dest.reshape(nt // bt, 1, bt * TOP_K)
    return pl.pallas_call(
        functools.partial(_final_kernel, bt=bt),
        grid=(n // bt,),
        in_specs=[pl.BlockSpec((1, 1, bt * TOP_K), lambda i: (b0 + i, 0, 0), memory_space=pltpu.SMEM),
                  pl.BlockSpec((bt, LANES), lambda i: (b0 + i, 0)),
                  pl.BlockSpec((bt, d), lambda i: (b0 + i, 0)),
                  pl.BlockSpec((None, None, 1, d), lambda i: (seq0 + i // bps, 5, 0, 0)),
                  pl.BlockSpec((1, d), lambda i: (0, 0)),
                  pl.BlockSpec(memory_space=pl.ANY)],
        out_specs=pl.BlockSpec((bt, d), lambda i: (i, 0)),
        out_shape=jax.ShapeDtypeStruct((n, d), F32),
        scratch_shapes=[pltpu.VMEM((TOP_K, bt, d), F32), pltpu.SemaphoreType.DMA(())],
        compiler_params=_cparams(("arbitrary",)),
        name="final",
    )(dest3, topw, x1_all, mod4, g_post, eo)


def kernel(x_prompt, x_sample, c_prompt, c_sample, w_ada, b_ada, g_pre_mix, g_post_mix, g_pre_ffn,
           g_post_ffn, w_in, w_out, q_norm, k_norm, lam_params, subln, w_router, b_router,
           w_gate_up, b_gate_up, w_down, b_down):
    bp, sp, d = x_prompt.shape
    bs, ss, _ = x_sample.shape
    assert bp == 1, "prompt group is a single sequence"
    n_p, n_s = bp * sp, bs * ss
    nt = n_p + n_s
    l = 0
    scale = HEAD_DIM ** -0.5

    w_in_b = w_in[l].astype(BF16)
    w_out_b = w_out[l].astype(BF16)
    w_gu_b = w_gate_up[l].astype(BF16)
    w_dn_b = w_down[l].astype(BF16)
    w_r_b = jnp.zeros((d, LANES), BF16).at[:, :N_EXPERTS].set(w_router[l].astype(BF16))
    b_r = jnp.full((1, LANES), NEG_BIG, F32).at[0, :N_EXPERTS].set(b_router[l].astype(F32))
    row = lambda g: g[l].astype(F32).reshape(1, -1)
    lp = lam_params[l].astype(F32)
    lam = (jnp.exp(jnp.sum(lp[0] * lp[1])) - jnp.exp(jnp.sum(lp[2] * lp[3])) + LAM_INIT).reshape(1)
    slopes = 2.0 ** (-8.0 * jnp.arange(1, B_HEADS + 1, dtype=F32) / B_HEADS)

    n_seq = bp + bs
    c_pad = jnp.zeros((16, d), F32).at[:n_seq].set(jnp.concatenate([c_prompt, c_sample], axis=0).astype(F32))
    mod = _ada(c_pad, w_ada[l], b_ada[l].astype(F32))
    mod4 = mod.reshape(16, N_MOD, 1, d)

    xp2 = x_prompt.reshape(n_p, d)
    xs2 = x_sample.reshape(n_s, d)
    groups = ((xp2, bp, sp, 0), (xs2, bs, ss, bp))
    mixes = []
    for x2, batch, seq_len, seq0 in groups:
        tables = (*_rope_tables(seq_len, q_norm[l], scale), *_rope_tables(seq_len, k_norm[l], 1.0))
        proj, c_a, c_b = _inproj(x2, mod4, row(g_pre_mix), w_in_b, tables, seq0=seq0, seq_len=seq_len)
        oa, ob = _attention(proj, c_a, c_b, slopes, lam, row(subln), batch, seq_len)
        mixes.append(_outmm(oa, ob, w_out_b))

    x1_all, h2p, logits = _post(mixes[0], mixes[1], xp2, xs2, mod4, row(g_post_mix), row(g_pre_ffn),
                                w_r_b, b_r, seq_s=ss, nseq_p=bp)

    idx, topw, rank, cnt = _route(logits)
    counts = cnt[0, :N_EXPERTS].astype(I32)
    padded = ((counts + MOE_BM - 1) // MOE_BM) * MOE_BM
    ends = jnp.cumsum(padded)
    pad_start = ends - padded
    dest = (pad_start[idx[:, :TOP_K]] + rank[:, :TOP_K]).astype(I32)
    n_blocks = (nt * TOP_K) // MOE_BM + N_EXPERTS
    total = n_blocks * MOE_BM
    block_starts = jnp.arange(n_blocks, dtype=I32) * MOE_BM
    block_e = jnp.clip(jnp.searchsorted(ends, block_starts, side="right"), 0, N_EXPERTS - 1).astype(I32)
    pad_info = jnp.stack([pad_start + counts, ends]).astype(I32)

    xs = _dispatch(pad_info, dest, h2p, total)
    block_valid = jnp.clip((pad_start + counts)[block_e] - block_starts, 0, MOE_BM).astype(I32)
    act = _gu(block_e, block_valid, xs, w_gu_b, b_gate_up[l].astype(F32))
    eo = _down(block_e, block_valid, act, w_dn_b, b_down[l].astype(F32))

    y_p = _final(dest, topw, x1_all, mod4, row(g_post_ffn), eo, row0=0, n=n_p, seq0=0, seq_len=sp)
    y_s = _final(dest, topw, x1_all, mod4, row(g_post_ffn), eo, row0=n_p, n=n_s, seq0=bp, seq_len=ss)
    return y_p.reshape(bp, sp, d), y_s.reshape(bs, ss, d)
```

```python
import functools
import math

import jax
import jax.numpy as jnp
from jax import lax
from jax.experimental import pallas as pl
from jax.experimental.pallas import tpu as pltpu

F32 = jnp.float32
BF16 = jnp.bfloat16
I32 = jnp.int32
U32 = jnp.uint32

HEAD_DIM = 128
GRID_W = 64
ROPE_THETA = 10000.0
RMS_EPS = 1e-6
A_HEADS = 16
A_KV_HEADS = 4
B_HEADS = 8
N_EXPERTS = 32
TOP_K = 4
SWIGLU_LIMIT = 7.0
SWIGLU_ALPHA = 1.702
N_MOD = 6
LAM_INIT = 0.8 - 0.6 * math.exp(-0.3 * 0)

LANES = 128
V7X_VMEM_BYTES = 64 * 1024 * 1024
VMEM_LIMIT = 56 * 1024 * 1024
NEG_BIG = -1e30

INPROJ_BM = 512
ATTN_A_BQ = 256
ATTN_A_BK = 512
ATTN_B_BQ = 512
ATTN_B_BK = 512
FAST_A_BQ = 256
FAST_A_BK = 2048
FAST_B_BQ = 512
FAST_B_BK = 512

SAFE_LOGIT = 40.0
EXP_ZERO_BELOW = -104.0
NORM_SLACK = 1.01
OUTMM_BM = 1024
OUTMM_BN = 512
POST_BM = 256
ROUTE_BT = 512
MOE_BM = 512
GU_TF = 512
DOWN_BN = 1024
DISPATCH_BT = 256
FINAL_BT = 128
ADA_BN = 512


def _cparams(sem):
    return pltpu.CompilerParams(dimension_semantics=("arbitrary",) * len(sem), vmem_limit_bytes=VMEM_LIMIT)


def _ada_kernel(c_ref, w_ref, b_ref, o_ref):
    c = c_ref[...]
    a = (c * jax.nn.sigmoid(c)).astype(BF16)
    o_ref[...] = jnp.dot(a, w_ref[...].astype(BF16), preferred_element_type=F32) + b_ref[...]


def _ada(c_pad, w_ada, b_ada):
    rows, d = c_pad.shape
    n = w_ada.shape[1]
    bn = min(ADA_BN, n)
    return pl.pallas_call(
        _ada_kernel,
        grid=(n // bn,),
        in_specs=[pl.BlockSpec((rows, d), lambda j: (0, 0)),
                  pl.BlockSpec((d, bn), lambda j: (0, j)),
                  pl.BlockSpec((1, bn), lambda j: (0, j))],
        out_specs=pl.BlockSpec((rows, bn), lambda j: (0, j)),
        out_shape=jax.ShapeDtypeStruct((rows, n), F32),
        compiler_params=_cparams(("parallel",)),
        name="ada",
    )(c_pad, w_ada, b_ada.reshape(1, n))


def _swap_pairs(x):
    n = x.shape[-1]
    lane = lax.broadcasted_iota(I32, x.shape, x.ndim - 1)
    up = pltpu.roll(x, n - 32, x.ndim - 1)
    dn = pltpu.roll(x, 32, x.ndim - 1)
    return jnp.where((lane % 64) < 32, up, dn)


def _inproj_kernel(x_ref, sc_ref, sh_ref, g_ref, w_ref, tqc_ref, tqs_ref, tkc_ref, tks_ref,
                   o_ref, n_ref, h_ref, *, nq, nk, qb_lo, qb_hi, kb_hi, heads_per_tile, qscale):
    j = pl.program_id(1)

    def max_sq_norm(y, best):
        n2 = jnp.max(jnp.sum(y * y, axis=-1, keepdims=True), axis=0, keepdims=True)
        return n2 if best is None else jnp.maximum(best, n2)

    @pl.when(j == 0)
    def _():
        x = x_ref[...]
        r = lax.rsqrt(jnp.mean(x * x, axis=-1, keepdims=True) + RMS_EPS)
        h = (x * r * g_ref[...]) * (1.0 + sc_ref[...]) + sh_ref[...]
        h_ref[...] = h.astype(BF16)

    acc = jnp.dot(h_ref[...], w_ref[...], preferred_element_type=F32)

    def put_norm(best):
        n_ref[...] = jnp.broadcast_to(best, n_ref.shape)

    def normed_rope(tc_ref, ts_ref):
        tc = tc_ref[...]
        ts = ts_ref[...]
        best = None
        for hh in range(heads_per_tile):
            xh = acc[:, hh * HEAD_DIM:(hh + 1) * HEAD_DIM]
            r = lax.rsqrt(jnp.mean(xh * xh, axis=-1, keepdims=True) + RMS_EPS)
            y = r * (xh * tc + _swap_pairs(xh) * ts)
            best = max_sq_norm(y, best)
            o_ref[:, hh * HEAD_DIM:(hh + 1) * HEAD_DIM] = y.astype(o_ref.dtype)
        put_norm(best)

    def scaled(factor):
        y = acc * factor
        best = None
        for hh in range(heads_per_tile):
            best = max_sq_norm(y[:, hh * HEAD_DIM:(hh + 1) * HEAD_DIM], best)
        o_ref[...] = y.astype(o_ref.dtype)
        put_norm(best)

    @pl.when(j < nq)
    def _():
        normed_rope(tqc_ref, tqs_ref)

    @pl.when(jnp.logical_and(j >= nq, j < nq + nk))
    def _():
        normed_rope(tkc_ref, tks_ref)

    @pl.when(jnp.logical_and(j >= qb_lo, j < qb_hi))
    def _():
        scaled(qscale)

    @pl.when(jnp.logical_and(j >= qb_hi, j < kb_hi))
    def _():
        scaled(1.0)

    @pl.when(jnp.logical_or(jnp.logical_and(j >= nq + nk, j < qb_lo), j >= kb_hi))
    def _():
        o_ref[...] = acc.astype(o_ref.dtype)
        n_ref[...] = jnp.zeros(n_ref.shape, F32)


def _inproj(x2d, mod4, g_pre, w_in, tables, *, seq0, seq_len):
    n, d = x2d.shape
    cols = w_in.shape[1]
    a_width = A_HEADS * HEAD_DIM
    kv_width = A_KV_HEADS * HEAD_DIM
    b_width = B_HEADS * 2 * HEAD_DIM
    bm = min(INPROJ_BM, seq_len)
    bn = min(512, kv_width)
    assert seq_len % bm == 0 and a_width % bn == 0 and kv_width % bn == 0 and b_width % bn == 0
    nq, nk = a_width // bn, kv_width // bn
    qb_lo = (a_width + 2 * kv_width) // bn
    qb_hi = qb_lo + b_width // bn
    kb_hi = qb_hi + b_width // bn
    blocks_per_seq = seq_len // bm
    seq_of = lambda i: seq0 + i // blocks_per_seq
    tab_spec = pl.BlockSpec((bm, HEAD_DIM), lambda i, j: (i % blocks_per_seq, 0))
    kernel = functools.partial(_inproj_kernel, nq=nq, nk=nk, qb_lo=qb_lo, qb_hi=qb_hi, kb_hi=kb_hi,
                               heads_per_tile=bn // HEAD_DIM, qscale=HEAD_DIM ** -0.5)
    proj, nrm = pl.pallas_call(
        kernel,
        grid=(n // bm, cols // bn),
        in_specs=[pl.BlockSpec((bm, d), lambda i, j: (i, 0)),
                  pl.BlockSpec((None, None, 1, d), lambda i, j: (seq_of(i), 1, 0, 0)),
                  pl.BlockSpec((None, None, 1, d), lambda i, j: (seq_of(i), 0, 0, 0)),
                  pl.BlockSpec((1, d), lambda i, j: (0, 0)),
                  pl.BlockSpec((d, bn), lambda i, j: (0, j)),
                  tab_spec, tab_spec, tab_spec, tab_spec],
        out_specs=[pl.BlockSpec((bm, bn), lambda i, j: (i, j)),
                   pl.BlockSpec((None, None, 8, LANES), lambda i, j: (i, j, 0, 0))],
        out_shape=[jax.ShapeDtypeStruct((n, cols), BF16),
                   jax.ShapeDtypeStruct((n // bm, cols // bn, 8, LANES), F32)],
        scratch_shapes=[pltpu.VMEM((bm, d), BF16)],
        compiler_params=_cparams(("parallel", "arbitrary")),
        name="inproj",
    )(x2d, mod4, mod4, g_pre, w_in, *tables)
    t = jnp.max(nrm[:, :, 0, 0], axis=0)
    bound = lambda qs, ks: jnp.sqrt(jnp.max(t[qs]) * jnp.max(t[ks])) * NORM_SLACK
    c_a = bound(slice(0, nq), slice(nq, nq + nk))
    c_b = bound(slice(qb_lo, qb_hi), slice(qb_hi, kb_hi))
    return proj, c_a, c_b


def _rope_tables(seq_len, gain, scale):
    half = HEAD_DIM // 2
    inv = ROPE_THETA ** (-jnp.arange(0, half, 2, dtype=F32) / half)
    t = jnp.arange(seq_len, dtype=I32)
    row = (t // GRID_W).astype(F32)
    col = (t % GRID_W).astype(F32)
    ar = row[:, None] * inv[None, :]
    ac = col[:, None] * inv[None, :]
    cos = jnp.concatenate([jnp.cos(ar), jnp.cos(ar), jnp.cos(ac), jnp.cos(ac)], axis=-1)
    sin = jnp.concatenate([-jnp.sin(ar), jnp.sin(ar), -jnp.sin(ac), jnp.sin(ac)], axis=-1)
    g = gain.astype(F32).reshape(HEAD_DIM)
    lane = jnp.arange(HEAD_DIM)
    partner = jnp.where((lane % 64) < 32, lane + 32, lane - 32)
    return cos * (g * scale)[None, :], sin * (g[partner] * scale)[None, :]


def _online_softmax_step(s, v, m_ref, l_ref, acc_ref):
    m_prev = m_ref[...]
    m_new = jnp.maximum(m_prev, jnp.max(s, axis=-1, keepdims=True))
    alpha = jnp.exp(m_prev - m_new)
    p = jnp.exp(s - m_new)
    l_ref[...] = alpha * l_ref[...] + jnp.sum(p, axis=-1, keepdims=True)
    acc_ref[...] = alpha * acc_ref[...] + jnp.dot(p.astype(BF16), v, preferred_element_type=F32)
    m_ref[...] = m_new


_NT = (((1,), (1,)), ((), ()))


def _attn_a_kernel(q_ref, k_ref, v_ref, o_ref, q4_ref, m_ref, l_ref, acc_ref, *, group, bq, bk, nkv):
    for h in range(group):
        q4_ref[h * bq:(h + 1) * bq, :] = q_ref[:, h * HEAD_DIM:(h + 1) * HEAD_DIM]
    m_ref[...] = jnp.full(m_ref.shape, -jnp.inf, F32)
    l_ref[...] = jnp.zeros(l_ref.shape, F32)
    acc_ref[...] = jnp.zeros(acc_ref.shape, F32)

    def body(j, carry):
        off = pl.multiple_of(j * bk, bk)
        k = k_ref[pl.ds(off, bk), :]
        v = v_ref[pl.ds(off, bk), :]
        s = lax.dot_general(q4_ref[...], k, _NT, preferred_element_type=F32)
        _online_softmax_step(s, v, m_ref, l_ref, acc_ref)
        return carry

    lax.fori_loop(0, nkv, body, 0)
    o = acc_ref[...] / l_ref[...]
    for h in range(group):
        o_ref[:, h * HEAD_DIM:(h + 1) * HEAD_DIM] = o[h * bq:(h + 1) * bq, :].astype(o_ref.dtype)


def _attn_a(proj, batch, seq_len):
    group = A_HEADS // A_KV_HEADS
    a_width = A_HEADS * HEAD_DIM
    kv_width = A_KV_HEADS * HEAD_DIM
    bq = min(ATTN_A_BQ, seq_len)
    bk = min(ATTN_A_BK, seq_len)
    qblocks = seq_len // bq
    gw = group * HEAD_DIM
    k_col0 = a_width // HEAD_DIM
    v_col0 = (a_width + kv_width) // HEAD_DIM
    rows = group * bq
    kernel = functools.partial(_attn_a_kernel, group=group, bq=bq, bk=bk, nkv=seq_len // bk)
    return pl.pallas_call(
        kernel,
        grid=(batch, A_KV_HEADS, qblocks),
        in_specs=[pl.BlockSpec((bq, gw), lambda b, g, i: (b * qblocks + i, g)),
                  pl.BlockSpec((seq_len, HEAD_DIM), lambda b, g, i: (b, k_col0 + g),
                               pipeline_mode=pl.Buffered(1)),
                  pl.BlockSpec((seq_len, HEAD_DIM), lambda b, g, i: (b, v_col0 + g),
                               pipeline_mode=pl.Buffered(1))],
        out_specs=pl.BlockSpec((bq, gw), lambda b, g, i: (b * qblocks + i, g)),
        out_shape=jax.ShapeDtypeStruct((batch * seq_len, a_width), BF16),
        scratch_shapes=[pltpu.VMEM((rows, HEAD_DIM), BF16),
                        pltpu.VMEM((rows, 1), F32),
                        pltpu.VMEM((rows, 1), F32),
                        pltpu.VMEM((rows, HEAD_DIM), F32)],
        compiler_params=_cparams(("parallel", "parallel", "parallel")),
        name="attn_a",
    )(proj, proj, proj)


def _attn_b_kernel(slope_ref, lam_ref, q_ref, k_ref, v_ref, subln_ref, o_ref, m_ref, l_ref, acc_ref,
                   *, bq, bk, nkv):
    h = pl.program_id(1)
    i = pl.program_id(2)
    slope = slope_ref[h]
    lam = lam_ref[0]
    q = q_ref[...]
    q1 = q[:, :HEAD_DIM]
    q2 = q[:, HEAD_DIM:]
    qpos = (i * bq + lax.broadcasted_iota(I32, (bq, 1), 0)).astype(F32)
    m_ref[...] = jnp.full(m_ref.shape, -jnp.inf, F32)
    l_ref[...] = jnp.zeros(l_ref.shape, F32)
    acc_ref[...] = jnp.zeros(acc_ref.shape, F32)

    def body(j, carry):
        off = pl.multiple_of(j * bk, bk)
        k = k_ref[pl.ds(off, bk), :]
        v = v_ref[pl.ds(off, bk), :]
        kpos = (j * bk + lax.broadcasted_iota(I32, (1, bk), 1)).astype(F32)
        bias = -slope * jnp.abs(qpos - kpos)
        s1 = lax.dot_general(q1, k[:, :HEAD_DIM], _NT, preferred_element_type=F32) + bias
        s2 = lax.dot_general(q2, k[:, HEAD_DIM:], _NT, preferred_element_type=F32) + bias
        _online_softmax_step(jnp.concatenate([s1, s2], axis=0), v, m_ref, l_ref, acc_ref)
        return carry

    lax.fori_loop(0, nkv, body, 0)
    o = acc_ref[...] / l_ref[...]
    o = o[:bq, :] - lam * o[bq:, :]
    r = lax.rsqrt(jnp.mean(o * o, axis=-1, keepdims=True) + RMS_EPS)
    o_ref[...] = ((o * r * subln_ref[...]) * (1.0 - LAM_INIT)).astype(o_ref.dtype)


def _attn_b(proj, slopes, lam, subln, batch, seq_len):
    a_width = A_HEADS * HEAD_DIM
    kv_width = A_KV_HEADS * HEAD_DIM
    b_width = B_HEADS * 2 * HEAD_DIM
    hw = 2 * HEAD_DIM
    bq = min(ATTN_B_BQ, seq_len)
    bk = min(ATTN_B_BK, seq_len)
    qblocks = seq_len // bq
    base = a_width + 2 * kv_width
    assert base % hw == 0
    q_col0, k_col0, v_col0 = base // hw, (base + b_width) // hw, (base + 2 * b_width) // hw
    kernel = functools.partial(_attn_b_kernel, bq=bq, bk=bk, nkv=seq_len // bk)
    grid_spec = pltpu.PrefetchScalarGridSpec(
        num_scalar_prefetch=2,
        grid=(batch, B_HEADS, qblocks),
        in_specs=[pl.BlockSpec((bq, hw), lambda b, h, i, *_: (b * qblocks + i, q_col0 + h)),
                  pl.BlockSpec((seq_len, hw), lambda b, h, i, *_: (b, k_col0 + h),
                               pipeline_mode=pl.Buffered(1)),
                  pl.BlockSpec((seq_len, hw), lambda b, h, i, *_: (b, v_col0 + h),
                               pipeline_mode=pl.Buffered(1)),
                  pl.BlockSpec((1, hw), lambda b, h, i, *_: (0, 0))],
        out_specs=pl.BlockSpec((bq, hw), lambda b, h, i, *_: (b * qblocks + i, h)),
        scratch_shapes=[pltpu.VMEM((2 * bq, 1), F32),
                        pltpu.VMEM((2 * bq, 1), F32),
                        pltpu.VMEM((2 * bq, hw), F32)],
    )
    return pl.pallas_call(
        kernel,
        grid_spec=grid_spec,
        out_shape=jax.ShapeDtypeStruct((batch * seq_len, b_width), BF16),
        compiler_params=_cparams(("parallel", "parallel", "parallel")),
        name="attn_b",
    )(slopes, lam, proj, proj, proj, subln)


def _lane_partial_sum(p):
    ps = p[:, 0:LANES]
    for t in range(1, p.shape[1] // LANES):
        ps = ps + p[:, t * LANES:(t + 1) * LANES]
    return ps


def _attn_a_fast_kernel(q_ref, k_ref, v_ref, o_ref, q4_ref, l_ref, acc_ref, *, group, bq, bk, nkv):
    for h in range(group):
        q4_ref[h * bq:(h + 1) * bq, :] = q_ref[:, h * HEAD_DIM:(h + 1) * HEAD_DIM]
    l_ref[...] = jnp.zeros(l_ref.shape, F32)
    acc_ref[...] = jnp.zeros(acc_ref.shape, F32)

    def body(j, carry):
        off = pl.multiple_of(j * bk, bk)
        k = k_ref[pl.ds(off, bk), :]
        v = v_ref[pl.ds(off, bk), :]
        p = jnp.exp(lax.dot_general(q4_ref[...], k, _NT, preferred_element_type=F32))
        l_ref[...] += _lane_partial_sum(p)
        acc_ref[...] += jnp.dot(p.astype(BF16), v, preferred_element_type=F32)
        return carry

    lax.fori_loop(0, nkv, body, 0)
    o = acc_ref[...] / jnp.sum(l_ref[...], axis=-1, keepdims=True)
    for h in range(group):
        o_ref[:, h * HEAD_DIM:(h + 1) * HEAD_DIM] = o[h * bq:(h + 1) * bq, :].astype(o_ref.dtype)


def _attn_a_fast(proj, batch, seq_len):
    group = A_HEADS // A_KV_HEADS
    a_width = A_HEADS * HEAD_DIM
    kv_width = A_KV_HEADS * HEAD_DIM
    bq = min(FAST_A_BQ, seq_len)
    bk = min(FAST_A_BK, seq_len)
    qblocks = seq_len // bq
    gw = group * HEAD_DIM
    k_col0 = a_width // HEAD_DIM
    v_col0 = (a_width + kv_width) // HEAD_DIM
    rows = group * bq
    kernel = functools.partial(_attn_a_fast_kernel, group=group, bq=bq, bk=bk, nkv=seq_len // bk)
    return pl.pallas_call(
        kernel,
        grid=(batch, A_KV_HEADS, qblocks),
        in_specs=[pl.BlockSpec((bq, gw), lambda b, g, i: (b * qblocks + i, g)),
                  pl.BlockSpec((seq_len, HEAD_DIM), lambda b, g, i: (b, k_col0 + g),
                               pipeline_mode=pl.Buffered(1)),
                  pl.BlockSpec((seq_len, HEAD_DIM), lambda b, g, i: (b, v_col0 + g),
                               pipeline_mode=pl.Buffered(1))],
        out_specs=pl.BlockSpec((bq, gw), lambda b, g, i: (b * qblocks + i, g)),
        out_shape=jax.ShapeDtypeStruct((batch * seq_len, a_width), BF16),
        scratch_shapes=[pltpu.VMEM((rows, HEAD_DIM), BF16),
                        pltpu.VMEM((rows, LANES), F32),
                        pltpu.VMEM((rows, HEAD_DIM), F32)],
        compiler_params=_cparams(("parallel", "parallel", "parallel")),
        name="attn_a_fast",
    )(proj, proj, proj)


def _attn_b_fast_kernel(slope_ref, lam_ref, reach_ref, q_ref, k_ref, v_ref, subln_ref, o_ref, l_ref, acc_ref,
                        p_ref, *, bq, bk, nkv):
    h = pl.program_id(1)
    i = pl.program_id(2)
    slope = slope_ref[h]
    lam = lam_ref[0]
    reach = reach_ref[h]
    q = q_ref[...]
    q1 = q[:, :HEAD_DIM]
    q2 = q[:, HEAD_DIM:]
    i0 = i * bq
    qpos = (i0 + lax.broadcasted_iota(I32, (bq, 1), 0)).astype(F32)
    l_ref[...] = jnp.zeros(l_ref.shape, F32)
    acc_ref[...] = jnp.zeros(acc_ref.shape, F32)
    jb_lo = jnp.maximum(i0 - reach, 0) // bk
    jb_hi = jnp.minimum((i0 + bq - 1 + reach) // bk + 1, nkv)

    odd = (jb_hi - jb_lo) % 2
    grow_hi = jnp.logical_and(odd == 1, jb_hi < nkv).astype(I32)
    jb_hi = jb_hi + grow_hi
    jb_lo = jb_lo - (odd - grow_hi)
    npair = (jb_hi - jb_lo) // 2

    def probs(j, slot):
        off = pl.multiple_of(j * bk, bk)
        k = k_ref[pl.ds(off, bk), :]
        kpos = (j * bk + lax.broadcasted_iota(I32, (1, bk), 1)).astype(F32)
        bias = -slope * jnp.abs(qpos - kpos)
        p1 = jnp.exp(lax.dot_general(q1, k[:, :HEAD_DIM], _NT, preferred_element_type=F32) + bias)
        p2 = jnp.exp(lax.dot_general(q2, k[:, HEAD_DIM:], _NT, preferred_element_type=F32) + bias)
        l_ref[:bq, :] += _lane_partial_sum(p1)
        l_ref[bq:, :] += _lane_partial_sum(p2)
        p_ref[slot, :bq, :] = p1.astype(BF16)
        p_ref[slot, bq:, :] = p2.astype(BF16)

    def weighted_values(j, slot):
        off = pl.multiple_of(j * bk, bk)
        acc_ref[...] += jnp.dot(p_ref[slot], v_ref[pl.ds(off, bk), :], preferred_element_type=F32)

    probs(jb_lo, 0)

    def body(t, carry):
        j = jb_lo + 2 * t
        weighted_values(j, 0)
        probs(j + 1, 1)
        weighted_values(j + 1, 1)
        probs(j + 2, 0)
        return carry

    lax.fori_loop(0, npair - 1, body, 0)
    weighted_values(jb_hi - 2, 0)
    probs(jb_hi - 1, 1)
    weighted_values(jb_hi - 1, 1)

    o = acc_ref[...] / jnp.sum(l_ref[...], axis=-1, keepdims=True)
    o = o[:bq, :] - lam * o[bq:, :]
    r = lax.rsqrt(jnp.mean(o * o, axis=-1, keepdims=True) + RMS_EPS)
    o_ref[...] = ((o * r * subln_ref[...]) * (1.0 - LAM_INIT)).astype(o_ref.dtype)


def _attn_b_fast(proj, slopes, lam, reach, subln, batch, seq_len):
    a_width = A_HEADS * HEAD_DIM
    kv_width = A_KV_HEADS * HEAD_DIM
    b_width = B_HEADS * 2 * HEAD_DIM
    hw = 2 * HEAD_DIM
    bq = min(FAST_B_BQ, seq_len)
    bk = min(FAST_B_BK, seq_len // 2)
    qblocks = seq_len // bq
    base = a_width + 2 * kv_width
    assert base % hw == 0 and (seq_len // bk) % 2 == 0
    q_col0, k_col0, v_col0 = base // hw, (base + b_width) // hw, (base + 2 * b_width) // hw
    kernel = functools.partial(_attn_b_fast_kernel, bq=bq, bk=bk, nkv=seq_len // bk)
    grid_spec = pltpu.PrefetchScalarGridSpec(
        num_scalar_prefetch=3,
        grid=(batch, B_HEADS, qblocks),
        in_specs=[pl.BlockSpec((bq, hw), lambda b, h, i, *_: (b * qblocks + i, q_col0 + h)),
                  pl.BlockSpec((seq_len, hw), lambda b, h, i, *_: (b, k_col0 + h),
                               pipeline_mode=pl.Buffered(1)),
                  pl.BlockSpec((seq_len, hw), lambda b, h, i, *_: (b, v_col0 + h),
                               pipeline_mode=pl.Buffered(1)),
                  pl.BlockSpec((1, hw), lambda b, h, i, *_: (0, 0))],
        out_specs=pl.BlockSpec((bq, hw), lambda b, h, i, *_: (b * qblocks + i, h)),
        scratch_shapes=[pltpu.VMEM((2 * bq, LANES), F32),
                        pltpu.VMEM((2 * bq, hw), F32),
                        pltpu.VMEM((2, 2 * bq, bk), BF16)],
    )
    return pl.pallas_call(
        kernel,
        grid_spec=grid_spec,
        out_shape=jax.ShapeDtypeStruct((batch * seq_len, b_width), BF16),
        compiler_params=_cparams(("parallel", "parallel", "parallel")),
        name="attn_b_fast",
    )(slopes, lam, reach, proj, proj, proj, subln)


def _attention(proj, c_a, c_b, slopes, lam, subln, batch, seq_len):
    oa = lax.cond(c_a <= SAFE_LOGIT,
                  lambda p: _attn_a_fast(p, batch, seq_len),
                  lambda p: _attn_a(p, batch, seq_len), proj)
    reach = jnp.clip(jnp.ceil((c_b - EXP_ZERO_BELOW) / slopes), 0, seq_len).astype(I32)
    ob = lax.cond(c_b <= SAFE_LOGIT,
                  lambda p: _attn_b_fast(p, slopes, lam, reach, subln, batch, seq_len),
                  lambda p: _attn_b(p, slopes, lam, subln, batch, seq_len), proj)
    return oa, ob


def _outmm_kernel(a1_ref, a2_ref, w1_ref, w2_ref, o_ref):
    o_ref[...] = (jnp.dot(a1_ref[...], w1_ref[...], preferred_element_type=F32)
                  + jnp.dot(a2_ref[...], w2_ref[...], preferred_element_type=F32))


def _outmm(oa, ob, w_out):
    n, ka = oa.shape
    kb = ob.shape[1]
    d = w_out.shape[1]
    bm = min(OUTMM_BM, n)
    bn = min(OUTMM_BN, d)
    assert ka == kb and ka % 16 == 0
    return pl.pallas_call(
        _outmm_kernel,
        grid=(n // bm, d // bn),
        in_specs=[pl.BlockSpec((bm, ka), lambda i, j: (i, 0)),
                  pl.BlockSpec((bm, kb), lambda i, j: (i, 0)),
                  pl.BlockSpec((ka, bn), lambda i, j: (0, j)),
                  pl.BlockSpec((kb, bn), lambda i, j: (1, j))],
        out_specs=pl.BlockSpec((bm, bn), lambda i, j: (i, j)),
        out_shape=jax.ShapeDtypeStruct((n, d), F32),
        compiler_params=_cparams(("parallel", "parallel")),
        name="outmm",
    )(oa, ob, w_out, w_out)


def _post_body(mix_ref, x_ref, gt_ref, sc_ref, sh_ref, gpost_ref, gpre_ref, wr_ref, br_ref,
               x1_ref, h2_ref, lg_ref):
    mix = mix_ref[...]
    r = lax.rsqrt(jnp.mean(mix * mix, axis=-1, keepdims=True) + RMS_EPS)
    x1 = x_ref[...] + gt_ref[...] * (mix * r * gpost_ref[...])
    x1_ref[...] = x1
    r2 = lax.rsqrt(jnp.mean(x1 * x1, axis=-1, keepdims=True) + RMS_EPS)
    h = (x1 * r2 * gpre_ref[...]) * (1.0 + sc_ref[...]) + sh_ref[...]
    hb = h.astype(BF16)
    half = h.shape[1] // 2
    hr = hb.astype(F32)
    lo = lax.bitcast_convert_type(hr[:, :half], U32) >> 16
    hi = lax.bitcast_convert_type(hr[:, half:], U32) & jnp.uint32(0xFFFF0000)
    h2_ref[...] = hi | lo
    lg_ref[...] = jnp.dot(hb, wr_ref[...], preferred_element_type=F32) + br_ref[...]


def _post_kernel(mixp_ref, mixs_ref, xp_ref, xs_ref, gt_ref, sc_ref, sh_ref, gpost_ref, gpre_ref,
                 wr_ref, br_ref, x1_ref, h2_ref, lg_ref, *, nbp):
    i = pl.program_id(0)
    rest = (gt_ref, sc_ref, sh_ref, gpost_ref, gpre_ref, wr_ref, br_ref, x1_ref, h2_ref, lg_ref)

    @pl.when(i < nbp)
    def _():
        _post_body(mixp_ref, xp_ref, *rest)

    @pl.when(i >= nbp)
    def _():
        _post_body(mixs_ref, xs_ref, *rest)


def _post(mix_p, mix_s, x_p, x_s, mod4, g_post, g_pre, w_router, b_router, *, seq_s, nseq_p):
    n_p, d = x_p.shape
    n_s = x_s.shape[0]
    bm = min(POST_BM, seq_s)
    assert n_p % bm == 0 and n_s % bm == 0 and seq_s % bm == 0
    nbp, nbs = n_p // bm, n_s // bm
    bps = seq_s // bm
    nt = n_p + n_s
    seq_of = lambda i: jnp.where(i < nbp, 0, nseq_p + (i - nbp) // bps)
    p_idx = lambda i: (jnp.minimum(i, nbp - 1), 0)
    s_idx = lambda i: (jnp.maximum(i - nbp, 0), 0)
    mod_spec = lambda which: pl.BlockSpec((None, None, 1, d), lambda i: (seq_of(i), which, 0, 0))
    row_spec = lambda c: pl.BlockSpec((bm, c), lambda i: (i, 0))
    return pl.pallas_call(
        functools.partial(_post_kernel, nbp=nbp),
        grid=(nbp + nbs,),
        in_specs=[pl.BlockSpec((bm, d), p_idx), pl.BlockSpec((bm, d), s_idx),
                  pl.BlockSpec((bm, d), p_idx), pl.BlockSpec((bm, d), s_idx),
                  mod_spec(2), mod_spec(4), mod_spec(3),
                  pl.BlockSpec((1, d), lambda i: (0, 0)),
                  pl.BlockSpec((1, d), lambda i: (0, 0)),
                  pl.BlockSpec((d, LANES), lambda i: (0, 0)),
                  pl.BlockSpec((1, LANES), lambda i: (0, 0))],
        out_specs=[row_spec(d), row_spec(d // 2), row_spec(LANES)],
        out_shape=[jax.ShapeDtypeStruct((nt, d), F32),
                   jax.ShapeDtypeStruct((nt, d // 2), U32),
                   jax.ShapeDtypeStruct((nt, LANES), F32)],
        compiler_params=_cparams(("parallel",)),
        name="post",
    )(mix_p, mix_s, x_p, x_s, mod4, mod4, mod4, g_post, g_pre, w_router, b_router)


def _route_kernel(lg_ref, idx_ref, w_ref, rank_ref, cnt_ref, carry_ref):
    i = pl.program_id(0)

    @pl.when(i == 0)
    def _():
        carry_ref[...] = jnp.zeros(carry_ref.shape, F32)

    lg = lg_ref[...]
    bt = lg.shape[0]
    lane = lax.broadcasted_iota(I32, lg.shape, 1)
    work = lg
    vals, idxs = [], []
    onehot = jnp.zeros(lg.shape, F32)
    for _ in range(TOP_K):
        mx = jnp.max(work, axis=-1, keepdims=True)
        ix = jnp.min(jnp.where(work == mx, lane, LANES), axis=-1, keepdims=True)
        sel = lane == ix
        vals.append(mx)
        idxs.append(ix)
        work = jnp.where(sel, -jnp.inf, work)
        onehot = onehot + sel.astype(F32)
    exps = [jnp.exp(v - vals[0]) for v in vals]
    denom = exps[0]
    for e in exps[1:]:
        denom = denom + e
    tri = (lax.broadcasted_iota(I32, (bt, bt), 0) > lax.broadcasted_iota(I32, (bt, bt), 1)).astype(BF16)
    prefix = jnp.dot(tri, onehot.astype(BF16), preferred_element_type=F32) + carry_ref[...]
    idx_out = jnp.zeros(lg.shape, I32)
    w_out = jnp.zeros(lg.shape, F32)
    rank_out = jnp.zeros(lg.shape, I32)
    for k in range(TOP_K):
        rk = jnp.sum(jnp.where(lane == idxs[k], prefix, 0.0), axis=-1, keepdims=True)
        idx_out = jnp.where(lane == k, idxs[k], idx_out)
        w_out = jnp.where(lane == k, exps[k] / denom, w_out)
        rank_out = jnp.where(lane == k, rk.astype(I32), rank_out)
    idx_ref[...] = idx_out
    w_ref[...] = w_out
    rank_ref[...] = rank_out
    carry_ref[...] = carry_ref[...] + jnp.sum(onehot, axis=0, keepdims=True)
    cnt_ref[...] = carry_ref[...]


def _route(logits):
    nt = logits.shape[0]
    bt = min(ROUTE_BT, nt)
    spec = pl.BlockSpec((bt, LANES), lambda i: (i, 0))
    return pl.pallas_call(
        _route_kernel,
        grid=(nt // bt,),
        in_specs=[spec],
        out_specs=[spec, spec, spec, pl.BlockSpec((1, LANES), lambda i: (0, 0))],
        out_shape=[jax.ShapeDtypeStruct((nt, LANES), I32),
                   jax.ShapeDtypeStruct((nt, LANES), F32),
                   jax.ShapeDtypeStruct((nt, LANES), I32),
                   jax.ShapeDtypeStruct((1, LANES), F32)],
        scratch_shapes=[pltpu.VMEM((1, LANES), F32)],
        compiler_params=_cparams(("arbitrary",)),
        name="route",
    )(logits)


def _dispatch_kernel(pad_ref, dest_ref, h_ref, xs_ref, zero_ref, sem, *, bt, n_experts, total):
    i = pl.program_id(0)

    def row_copy(src, r, d):
        return pltpu.make_async_copy(src.at[pl.ds(r, 1), :], xs_ref.at[pl.ds(d, 1), :], sem)

    @pl.when(i == 0)
    def _():
        zero_ref[...] = jnp.zeros(zero_ref.shape, zero_ref.dtype)

        def fill(lo, hi):
            def start(r, c):
                row_copy(zero_ref, 0, r).start()
                return c

            def wait(r, c):
                row_copy(zero_ref, 0, r).wait()
                return c

            lax.fori_loop(lo, hi, start, 0)
            lax.fori_loop(lo, hi, wait, 0)

        def per_expert(e, c):
            fill(pad_ref[0, e], pad_ref[1, e])
            return c

        lax.fori_loop(0, n_experts, per_expert, 0)
        fill(pad_ref[1, n_experts - 1], total)

    def start(t, c):
        for k in range(TOP_K):
            row_copy(h_ref, t, dest_ref[0, 0, t * TOP_K + k]).start(priority=k % 2)
        return c

    def wait(t, c):
        for k in range(TOP_K):
            row_copy(h_ref, t, dest_ref[0, 0, t * TOP_K + k]).wait()
        return c

    lax.fori_loop(0, bt, start, 0)
    lax.fori_loop(0, bt, wait, 0)


def _dispatch(pad_info, dest, h2p, total):
    nt, half = h2p.shape
    bt = min(DISPATCH_BT, nt)
    dest3 = dest.reshape(nt // bt, 1, bt * TOP_K)
    kernel = functools.partial(_dispatch_kernel, bt=bt, n_experts=N_EXPERTS, total=total)
    grid_spec = pltpu.PrefetchScalarGridSpec(
        num_scalar_prefetch=1,
        grid=(nt // bt,),
        in_specs=[pl.BlockSpec((1, 1, bt * TOP_K), lambda i, *_: (i, 0, 0), memory_space=pltpu.SMEM),
                  pl.BlockSpec((bt, half), lambda i, *_: (i, 0))],
        out_specs=pl.BlockSpec(memory_space=pl.ANY),
        scratch_shapes=[pltpu.VMEM((8, half), U32), pltpu.SemaphoreType.DMA(())],
    )
    return pl.pallas_call(
        kernel,
        grid_spec=grid_spec,
        out_shape=jax.ShapeDtypeStruct((total, half), U32),
        compiler_params=_cparams(("arbitrary",)),
        name="dispatch",
    )(pad_info, dest3, h2p)


def _unpack_rows(p):
    lo = lax.bitcast_convert_type(p << 16, F32)
    hi = lax.bitcast_convert_type(p & jnp.uint32(0xFFFF0000), F32)
    return lo.astype(BF16), hi.astype(BF16)


def _gu_kernel(be_ref, bv_ref, xs_ref, wg_ref, wu_ref, bg_ref, bu_ref, o_ref):
    @pl.when(bv_ref[pl.program_id(1)] > 0)
    def _():
        lo, hi = _unpack_rows(xs_ref[...])
        half = lo.shape[1]

        def proj(w_ref, b_ref):
            return (jnp.dot(lo, w_ref[:half, :].astype(BF16), preferred_element_type=F32)
                    + jnp.dot(hi, w_ref[half:, :].astype(BF16), preferred_element_type=F32) + b_ref[...])

        g = jnp.minimum(proj(wg_ref, bg_ref), SWIGLU_LIMIT)
        u = jnp.clip(proj(wu_ref, bu_ref), -SWIGLU_LIMIT, SWIGLU_LIMIT)
        o_ref[...] = ((u + 1.0) * (g * jax.nn.sigmoid(SWIGLU_ALPHA * g))).astype(o_ref.dtype)


def _gu(block_e, block_valid, xs, w_gu, b_gu):
    total, half = xs.shape
    d = 2 * half
    n_e, _, two_f = w_gu.shape
    f = two_f // 2
    bm = MOE_BM
    tf = min(GU_TF, f)
    nft = f // tf
    grid_spec = pltpu.PrefetchScalarGridSpec(
        num_scalar_prefetch=2,
        grid=(nft, total // bm),
        in_specs=[pl.BlockSpec((bm, half), lambda j, i, be, bv: (i, 0)),
                  pl.BlockSpec((None, d, tf), lambda j, i, be, bv: (be[i], 0, j)),
                  pl.BlockSpec((None, d, tf), lambda j, i, be, bv: (be[i], 0, nft + j)),
                  pl.BlockSpec((None, 1, tf), lambda j, i, be, bv: (be[i], 0, j)),
                  pl.BlockSpec((None, 1, tf), lambda j, i, be, bv: (be[i], 0, nft + j))],
        out_specs=pl.BlockSpec((bm, tf), lambda j, i, be, bv: (i, j)),
    )
    return pl.pallas_call(
        _gu_kernel,
        grid_spec=grid_spec,
        out_shape=jax.ShapeDtypeStruct((total, f), BF16),
        compiler_params=_cparams(("parallel", "parallel")),
        name="moe_gu",
    )(block_e, block_valid, xs, w_gu, w_gu, b_gu.reshape(n_e, 1, two_f), b_gu.reshape(n_e, 1, two_f))


def _down_kernel(be_ref, bv_ref, a_ref, w_ref, b_ref, o_ref):
    @pl.when(bv_ref[pl.program_id(1)] > 0)
    def _():
        o_ref[...] = jnp.dot(a_ref[...], w_ref[...].astype(BF16), preferred_element_type=F32) + b_ref[...]


def _down(block_e, block_valid, act, w_down, b_down):
    total, f = act.shape
    n_e, _, d = w_down.shape
    bm = MOE_BM
    bn = min(DOWN_BN, d)
    grid_spec = pltpu.PrefetchScalarGridSpec(
        num_scalar_prefetch=2,
        grid=(d // bn, total // bm),
        in_specs=[pl.BlockSpec((bm, f), lambda j, i, be, bv: (i, 0)),
                  pl.BlockSpec((None, f, bn), lambda j, i, be, bv: (be[i], 0, j)),
                  pl.BlockSpec((None, 1, bn), lambda j, i, be, bv: (be[i], 0, j))],
        out_specs=pl.BlockSpec((bm, bn), lambda j, i, be, bv: (i, j)),
    )
    return pl.pallas_call(
        _down_kernel,
        grid_spec=grid_spec,
        out_shape=jax.ShapeDtypeStruct((total, d), F32),
        compiler_params=_cparams(("parallel", "parallel")),
        name="moe_down",
    )(block_e, block_valid, act, w_down, b_down.reshape(n_e, 1, d))


def _final_kernel(dest_ref, w_ref, x1_ref, gt_ref, g_ref, eo_ref, y_ref, rows_ref, sem, *, bt):
    def row_copy(t, k):
        d = dest_ref[0, 0, t * TOP_K + k]
        return pltpu.make_async_copy(eo_ref.at[pl.ds(d, 1), :], rows_ref.at[k, pl.ds(t, 1), :], sem)

    def start(t, c):
        for k in range(TOP_K):
            row_copy(t, k).start()
        return c

    def wait(t, c):
        for k in range(TOP_K):
            row_copy(t, k).wait()
        return c

    lax.fori_loop(0, bt, start, 0)
    lax.fori_loop(0, bt, wait, 0)
    w = w_ref[...]
    f = w[:, 0:1] * rows_ref[0]
    for k in range(1, TOP_K):
        f = f + w[:, k:k + 1] * rows_ref[k]
    r = lax.rsqrt(jnp.mean(f * f, axis=-1, keepdims=True) + RMS_EPS)
    y_ref[...] = x1_ref[...] + gt_ref[...] * (f * r * g_ref[...])


def _final(dest, topw, x1_all, mod4, g_post, eo, *, row0, n, seq0, seq_len):
    d = x1_all.shape[1]
    bt = min(FINAL_BT, seq_len)
    assert row0 % bt == 0 and n % bt == 0 and seq_len % bt == 0
    b0 = row0 // bt
    bps = seq_len // bt
    nt = dest.shape[0]
    dest3 = dest.reshape(nt // bt, 1, bt * TOP_K)
    return pl.pallas_call(
        functools.partial(_final_kernel, bt=bt),
        grid=(n // bt,),
        in_specs=[pl.BlockSpec((1, 1, bt * TOP_K), lambda i: (b0 + i, 0, 0), memory_space=pltpu.SMEM),
                  pl.BlockSpec((bt, LANES), lambda i: (b0 + i, 0)),
                  pl.BlockSpec((bt, d), lambda i: (b0 + i, 0)),
                  pl.BlockSpec((None, None, 1, d), lambda i: (seq0 + i // bps, 5, 0, 0)),
                  pl.BlockSpec((1, d), lambda i: (0, 0)),
                  pl.BlockSpec(memory_space=pl.ANY)],
        out_specs=pl.BlockSpec((bt, d), lambda i: (i, 0)),
        out_shape=jax.ShapeDtypeStruct((n, d), F32),
        scratch_shapes=[pltpu.VMEM((TOP_K, bt, d), F32), pltpu.SemaphoreType.DMA(())],
        compiler_params=_cparams(("arbitrary",)),
        name="final",
    )(dest3, topw, x1_all, mod4, g_post, eo)


def kernel(x_prompt, x_sample, c_prompt, c_sample, w_ada, b_ada, g_pre_mix, g_post_mix, g_pre_ffn,
           g_post_ffn, w_in, w_out, q_norm, k_norm, lam_params, subln, w_router, b_router,
           w_gate_up, b_gate_up, w_down, b_down):
    bp, sp, d = x_prompt.shape
    bs, ss, _ = x_sample.shape
    assert bp == 1, "prompt group is a single sequence"
    n_p, n_s = bp * sp, bs * ss
    nt = n_p + n_s
    l = 0
    scale = HEAD_DIM ** -0.5

    w_in_b = w_in[l].astype(BF16)
    w_out_b = w_out[l].astype(BF16)
    w_r_b = jnp.zeros((d, LANES), BF16).at[:, :N_EXPERTS].set(w_router[l].astype(BF16))
    b_r = jnp.full((1, LANES), NEG_BIG, F32).at[0, :N_EXPERTS].set(b_router[l].astype(F32))
    row = lambda g: g[l].astype(F32).reshape(1, -1)
    lp = lam_params[l].astype(F32)
    lam = (jnp.exp(jnp.sum(lp[0] * lp[1])) - jnp.exp(jnp.sum(lp[2] * lp[3])) + LAM_INIT).reshape(1)
    slopes = 2.0 ** (-8.0 * jnp.arange(1, B_HEADS + 1, dtype=F32) / B_HEADS)

    n_seq = bp + bs
    c_pad = jnp.zeros((16, d), F32).at[:n_seq].set(jnp.concatenate([c_prompt, c_sample], axis=0).astype(F32))
    mod = _ada(c_pad, w_ada[l], b_ada[l].astype(F32))
    mod4 = mod.reshape(16, N_MOD, 1, d)

    xp2 = x_prompt.reshape(n_p, d)
    xs2 = x_sample.reshape(n_s, d)
    groups = ((xp2, bp, sp, 0), (xs2, bs, ss, bp))
    mixes = []
    for x2, batch, seq_len, seq0 in groups:
        tables = (*_rope_tables(seq_len, q_norm[l], scale), *_rope_tables(seq_len, k_norm[l], 1.0))
        proj, c_a, c_b = _inproj(x2, mod4, row(g_pre_mix), w_in_b, tables, seq0=seq0, seq_len=seq_len)
        oa, ob = _attention(proj, c_a, c_b, slopes, lam, row(subln), batch, seq_len)
        mixes.append(_outmm(oa, ob, w_out_b))

    x1_all, h2p, logits = _post(mixes[0], mixes[1], xp2, xs2, mod4, row(g_post_mix), row(g_pre_ffn),
                                w_r_b, b_r, seq_s=ss, nseq_p=bp)

    idx, topw, rank, cnt = _route(logits)
    counts = cnt[0, :N_EXPERTS].astype(I32)
    padded = ((counts + MOE_BM - 1) // MOE_BM) * MOE_BM
    ends = jnp.cumsum(padded)
    pad_start = ends - padded
    dest = (pad_start[idx[:, :TOP_K]] + rank[:, :TOP_K]).astype(I32)
    n_blocks = (nt * TOP_K) // MOE_BM + N_EXPERTS
    total = n_blocks * MOE_BM
    block_starts = jnp.arange(n_blocks, dtype=I32) * MOE_BM
    block_e = jnp.clip(jnp.searchsorted(ends, block_starts, side="right"), 0, N_EXPERTS - 1).astype(I32)
    pad_info = jnp.stack([pad_start + counts, ends]).astype(I32)

    xs = _dispatch(pad_info, dest, h2p, total)
    block_valid = jnp.clip((pad_start + counts)[block_e] - block_starts, 0, MOE_BM).astype(I32)
    assert w_gate_up.shape[0] == 1 and w_down.shape[0] == 1, "single-layer stack"
    act = _gu(block_e, block_valid, xs, w_gate_up.reshape(w_gate_up.shape[1:]), b_gate_up[l].astype(F32))
    eo = _down(block_e, block_valid, act, w_down.reshape(w_down.shape[1:]), b_down[l].astype(F32))

    y_p = _final(dest, topw, x1_all, mod4, row(g_post_ffn), eo, row0=0, n=n_p, seq0=0, seq_len=sp)
    y_s = _final(dest, topw, x1_all, mod4, row(g_post_ffn), eo, row0=n_p, n=n_s, seq0=bp, seq_len=ss)
    return y_p.reshape(bp, sp, d), y_s.reshape(bs, ss, d)
```

```python
import functools
import math

import jax
import jax.numpy as jnp
import numpy as np
from jax import lax
from jax.experimental import pallas as pl
from jax.experimental.pallas import tpu as pltpu

F32 = jnp.float32
BF16 = jnp.bfloat16
I32 = jnp.int32
U32 = jnp.uint32

HEAD_DIM = 128
GRID_W = 64
ROPE_THETA = 10000.0
RMS_EPS = 1e-6
A_HEADS = 16
A_KV_HEADS = 4
B_HEADS = 8
N_EXPERTS = 32
TOP_K = 4
SWIGLU_LIMIT = 7.0
SWIGLU_ALPHA = 1.702
N_MOD = 6
LAM_INIT = 0.8 - 0.6 * math.exp(-0.3 * 0)

LANES = 128
V7X_VMEM_BYTES = 64 * 1024 * 1024
VMEM_LIMIT = 56 * 1024 * 1024
NEG_BIG = -1e30

INPROJ_BM = 512
ATTN_A_BQ = 256
ATTN_A_BK = 512
ATTN_B_BQ = 512
ATTN_B_BK = 512
FAST_A_BQ = 256
FAST_A_BK = 2048
FAST_B_BQ = 512
FAST_B_BK = 512

SAFE_LOGIT = 40.0
EXP_ZERO_BELOW = -104.0
NORM_SLACK = 1.01
OUTMM_BM = 1024
OUTMM_BN = 512
POST_BM = 256
ROUTE_BT = 512
MOE_BM = 512
GU_TF = 512
DOWN_BN = 1024
DISPATCH_BT = 256
FINAL_BT = 256
ADA_BN = 512


def _cparams(sem):
    return pltpu.CompilerParams(dimension_semantics=("arbitrary",) * len(sem), vmem_limit_bytes=VMEM_LIMIT)


def _ada_kernel(c_ref, w_ref, b_ref, o_ref):
    c = c_ref[...]
    a = (c * jax.nn.sigmoid(c)).astype(BF16)
    o_ref[...] = jnp.dot(a, w_ref[...].astype(BF16), preferred_element_type=F32) + b_ref[...]


def _ada(c_pad, w_ada, b_ada):
    rows, d = c_pad.shape
    n = w_ada.shape[1]
    bn = min(ADA_BN, n)
    return pl.pallas_call(
        _ada_kernel,
        grid=(n // bn,),
        in_specs=[pl.BlockSpec((rows, d), lambda j: (0, 0)),
                  pl.BlockSpec((d, bn), lambda j: (0, j)),
                  pl.BlockSpec((1, bn), lambda j: (0, j))],
        out_specs=pl.BlockSpec((rows, bn), lambda j: (0, j)),
        out_shape=jax.ShapeDtypeStruct((rows, n), F32),
        compiler_params=_cparams(("parallel",)),
        name="ada",
    )(c_pad, w_ada, b_ada.reshape(1, n))


def _swap_pairs(x):
    n = x.shape[-1]
    lane = lax.broadcasted_iota(I32, x.shape, x.ndim - 1)
    up = pltpu.roll(x, n - 32, x.ndim - 1)
    dn = pltpu.roll(x, 32, x.ndim - 1)
    return jnp.where((lane % 64) < 32, up, dn)


def _inproj_kernel(x_ref, sc_ref, sh_ref, g_ref, w_ref, tqc_ref, tqs_ref, tkc_ref, tks_ref,
                   o_ref, n_ref, h_ref, *, nq, nk, qb_lo, qb_hi, kb_hi, heads_per_tile, qscale):
    j = pl.program_id(1)

    def max_sq_norm(y, best):
        n2 = jnp.max(jnp.sum(y * y, axis=-1, keepdims=True), axis=0, keepdims=True)
        return n2 if best is None else jnp.maximum(best, n2)

    @pl.when(j == 0)
    def _():
        x = x_ref[...]
        r = lax.rsqrt(jnp.mean(x * x, axis=-1, keepdims=True) + RMS_EPS)
        h = (x * r * g_ref[...]) * (1.0 + sc_ref[...]) + sh_ref[...]
        h_ref[...] = h.astype(BF16)

    acc = jnp.dot(h_ref[...], w_ref[...], preferred_element_type=F32)

    def put_norm(best):
        n_ref[...] = jnp.broadcast_to(best, n_ref.shape)

    def normed_rope(tc_ref, ts_ref):
        tc = tc_ref[...]
        ts = ts_ref[...]
        best = None
        for hh in range(heads_per_tile):
            xh = acc[:, hh * HEAD_DIM:(hh + 1) * HEAD_DIM]
            r = lax.rsqrt(jnp.mean(xh * xh, axis=-1, keepdims=True) + RMS_EPS)
            y = r * (xh * tc + _swap_pairs(xh) * ts)
            best = max_sq_norm(y, best)
            o_ref[:, hh * HEAD_DIM:(hh + 1) * HEAD_DIM] = y.astype(o_ref.dtype)
        put_norm(best)

    def scaled(factor):
        y = acc * factor
        best = None
        for hh in range(heads_per_tile):
            best = max_sq_norm(y[:, hh * HEAD_DIM:(hh + 1) * HEAD_DIM], best)
        o_ref[...] = y.astype(o_ref.dtype)
        put_norm(best)

    @pl.when(j < nq)
    def _():
        normed_rope(tqc_ref, tqs_ref)

    @pl.when(jnp.logical_and(j >= nq, j < nq + nk))
    def _():
        normed_rope(tkc_ref, tks_ref)

    @pl.when(jnp.logical_and(j >= qb_lo, j < qb_hi))
    def _():
        scaled(qscale)

    @pl.when(jnp.logical_and(j >= qb_hi, j < kb_hi))
    def _():
        scaled(1.0)

    @pl.when(jnp.logical_or(jnp.logical_and(j >= nq + nk, j < qb_lo), j >= kb_hi))
    def _():
        o_ref[...] = acc.astype(o_ref.dtype)
        n_ref[...] = jnp.zeros(n_ref.shape, F32)


def _inproj(x2d, mod4, g_pre, w_in, tables, *, seq0, seq_len):
    n, d = x2d.shape
    cols = w_in.shape[1]
    a_width = A_HEADS * HEAD_DIM
    kv_width = A_KV_HEADS * HEAD_DIM
    b_width = B_HEADS * 2 * HEAD_DIM
    bm = min(INPROJ_BM, seq_len)
    bn = min(512, kv_width)
    assert seq_len % bm == 0 and a_width % bn == 0 and kv_width % bn == 0 and b_width % bn == 0
    nq, nk = a_width // bn, kv_width // bn
    qb_lo = (a_width + 2 * kv_width) // bn
    qb_hi = qb_lo + b_width // bn
    kb_hi = qb_hi + b_width // bn
    blocks_per_seq = seq_len // bm
    seq_of = lambda i: seq0 + i // blocks_per_seq
    tab_spec = pl.BlockSpec((bm, HEAD_DIM), lambda i, j: (i % blocks_per_seq, 0))
    kernel = functools.partial(_inproj_kernel, nq=nq, nk=nk, qb_lo=qb_lo, qb_hi=qb_hi, kb_hi=kb_hi,
                               heads_per_tile=bn // HEAD_DIM, qscale=HEAD_DIM ** -0.5)
    proj, nrm = pl.pallas_call(
        kernel,
        grid=(n // bm, cols // bn),
        in_specs=[pl.BlockSpec((bm, d), lambda i, j: (i, 0)),
                  pl.BlockSpec((None, None, 1, d), lambda i, j: (seq_of(i), 1, 0, 0)),
                  pl.BlockSpec((None, None, 1, d), lambda i, j: (seq_of(i), 0, 0, 0)),
                  pl.BlockSpec((1, d), lambda i, j: (0, 0)),
                  pl.BlockSpec((d, bn), lambda i, j: (0, j)),
                  tab_spec, tab_spec, tab_spec, tab_spec],
        out_specs=[pl.BlockSpec((bm, bn), lambda i, j: (i, j)),
                   pl.BlockSpec((None, None, 8, LANES), lambda i, j: (i, j, 0, 0))],
        out_shape=[jax.ShapeDtypeStruct((n, cols), BF16),
                   jax.ShapeDtypeStruct((n // bm, cols // bn, 8, LANES), F32)],
        scratch_shapes=[pltpu.VMEM((bm, d), BF16)],
        compiler_params=_cparams(("parallel", "arbitrary")),
        name="inproj",
    )(x2d, mod4, mod4, g_pre, w_in, *tables)
    t = jnp.max(nrm[:, :, 0, 0], axis=0)
    bound = lambda qs, ks: jnp.sqrt(jnp.max(t[qs]) * jnp.max(t[ks])) * NORM_SLACK
    c_a = bound(slice(0, nq), slice(nq, nq + nk))
    c_b = bound(slice(qb_lo, qb_hi), slice(qb_hi, kb_hi))
    return proj, c_a, c_b


def _rope_tables(seq_len, gain, scale):
    half = HEAD_DIM // 2
    n_rows = seq_len // GRID_W
    inv = np.float32(ROPE_THETA) ** (-np.arange(0, half, 2, dtype=np.float32) / np.float32(half))
    ar = (np.arange(n_rows, dtype=np.float32)[:, None] * inv[None, :]).astype(np.float64)
    ac = (np.arange(GRID_W, dtype=np.float32)[:, None] * inv[None, :]).astype(np.float64)
    per_row = lambda a: jnp.broadcast_to(jnp.asarray(a, F32)[:, None, :], (n_rows, GRID_W, half // 2))
    per_col = lambda a: jnp.broadcast_to(jnp.asarray(a, F32)[None, :, :], (n_rows, GRID_W, half // 2))
    cr, sr, cc, sc = per_row(np.cos(ar)), per_row(np.sin(ar)), per_col(np.cos(ac)), per_col(np.sin(ac))
    cos = jnp.concatenate([cr, cr, cc, cc], axis=-1).reshape(seq_len, HEAD_DIM)
    sin = jnp.concatenate([-sr, sr, -sc, sc], axis=-1).reshape(seq_len, HEAD_DIM)
    g = gain.astype(F32).reshape(HEAD_DIM)
    lane = jnp.arange(HEAD_DIM)
    partner = jnp.where((lane % 64) < 32, lane + 32, lane - 32)
    return cos * (g * scale)[None, :], sin * (g[partner] * scale)[None, :]


def _online_softmax_step(s, v, m_ref, l_ref, acc_ref):
    m_prev = m_ref[...]
    m_new = jnp.maximum(m_prev, jnp.max(s, axis=-1, keepdims=True))
    alpha = jnp.exp(m_prev - m_new)
    p = jnp.exp(s - m_new)
    l_ref[...] = alpha * l_ref[...] + jnp.sum(p, axis=-1, keepdims=True)
    acc_ref[...] = alpha * acc_ref[...] + jnp.dot(p.astype(BF16), v, preferred_element_type=F32)
    m_ref[...] = m_new


_NT = (((1,), (1,)), ((), ()))


def _attn_a_kernel(q_ref, k_ref, v_ref, o_ref, q4_ref, m_ref, l_ref, acc_ref, *, group, bq, bk, nkv):
    for h in range(group):
        q4_ref[h * bq:(h + 1) * bq, :] = q_ref[:, h * HEAD_DIM:(h + 1) * HEAD_DIM]
    m_ref[...] = jnp.full(m_ref.shape, -jnp.inf, F32)
    l_ref[...] = jnp.zeros(l_ref.shape, F32)
    acc_ref[...] = jnp.zeros(acc_ref.shape, F32)

    def body(j, carry):
        off = pl.multiple_of(j * bk, bk)
        k = k_ref[pl.ds(off, bk), :]
        v = v_ref[pl.ds(off, bk), :]
        s = lax.dot_general(q4_ref[...], k, _NT, preferred_element_type=F32)
        _online_softmax_step(s, v, m_ref, l_ref, acc_ref)
        return carry

    lax.fori_loop(0, nkv, body, 0)
    o = acc_ref[...] / l_ref[...]
    for h in range(group):
        o_ref[:, h * HEAD_DIM:(h + 1) * HEAD_DIM] = o[h * bq:(h + 1) * bq, :].astype(o_ref.dtype)


def _attn_a(proj, batch, seq_len):
    group = A_HEADS // A_KV_HEADS
    a_width = A_HEADS * HEAD_DIM
    kv_width = A_KV_HEADS * HEAD_DIM
    bq = min(ATTN_A_BQ, seq_len)
    bk = min(ATTN_A_BK, seq_len)
    qblocks = seq_len // bq
    gw = group * HEAD_DIM
    k_col0 = a_width // HEAD_DIM
    v_col0 = (a_width + kv_width) // HEAD_DIM
    rows = group * bq
    kernel = functools.partial(_attn_a_kernel, group=group, bq=bq, bk=bk, nkv=seq_len // bk)
    return pl.pallas_call(
        kernel,
        grid=(batch, A_KV_HEADS, qblocks),
        in_specs=[pl.BlockSpec((bq, gw), lambda b, g, i: (b * qblocks + i, g)),
                  pl.BlockSpec((seq_len, HEAD_DIM), lambda b, g, i: (b, k_col0 + g),
                               pipeline_mode=pl.Buffered(1)),
                  pl.BlockSpec((seq_len, HEAD_DIM), lambda b, g, i: (b, v_col0 + g),
                               pipeline_mode=pl.Buffered(1))],
        out_specs=pl.BlockSpec((bq, gw), lambda b, g, i: (b * qblocks + i, g)),
        out_shape=jax.ShapeDtypeStruct((batch * seq_len, a_width), BF16),
        scratch_shapes=[pltpu.VMEM((rows, HEAD_DIM), BF16),
                        pltpu.VMEM((rows, 1), F32),
                        pltpu.VMEM((rows, 1), F32),
                        pltpu.VMEM((rows, HEAD_DIM), F32)],
        compiler_params=_cparams(("parallel", "parallel", "parallel")),
        name="attn_a",
    )(proj, proj, proj)


def _attn_b_kernel(slope_ref, lam_ref, q_ref, k_ref, v_ref, subln_ref, o_ref, m_ref, l_ref, acc_ref,
                   *, bq, bk, nkv):
    h = pl.program_id(1)
    i = pl.program_id(2)
    slope = slope_ref[h]
    lam = lam_ref[0]
    q = q_ref[...]
    q1 = q[:, :HEAD_DIM]
    q2 = q[:, HEAD_DIM:]
    qpos = (i * bq + lax.broadcasted_iota(I32, (bq, 1), 0)).astype(F32)
    m_ref[...] = jnp.full(m_ref.shape, -jnp.inf, F32)
    l_ref[...] = jnp.zeros(l_ref.shape, F32)
    acc_ref[...] = jnp.zeros(acc_ref.shape, F32)

    def body(j, carry):
        off = pl.multiple_of(j * bk, bk)
        k = k_ref[pl.ds(off, bk), :]
        v = v_ref[pl.ds(off, bk), :]
        kpos = (j * bk + lax.broadcasted_iota(I32, (1, bk), 1)).astype(F32)
        bias = -slope * jnp.abs(qpos - kpos)
        s1 = lax.dot_general(q1, k[:, :HEAD_DIM], _NT, preferred_element_type=F32) + bias
        s2 = lax.dot_general(q2, k[:, HEAD_DIM:], _NT, preferred_element_type=F32) + bias
        _online_softmax_step(jnp.concatenate([s1, s2], axis=0), v, m_ref, l_ref, acc_ref)
        return carry

    lax.fori_loop(0, nkv, body, 0)
    o = acc_ref[...] / l_ref[...]
    o = o[:bq, :] - lam * o[bq:, :]
    r = lax.rsqrt(jnp.mean(o * o, axis=-1, keepdims=True) + RMS_EPS)
    o_ref[...] = ((o * r * subln_ref[...]) * (1.0 - LAM_INIT)).astype(o_ref.dtype)


def _attn_b(proj, slopes, lam, subln, batch, seq_len):
    a_width = A_HEADS * HEAD_DIM
    kv_width = A_KV_HEADS * HEAD_DIM
    b_width = B_HEADS * 2 * HEAD_DIM
    hw = 2 * HEAD_DIM
    bq = min(ATTN_B_BQ, seq_len)
    bk = min(ATTN_B_BK, seq_len)
    qblocks = seq_len // bq
    base = a_width + 2 * kv_width
    assert base % hw == 0
    q_col0, k_col0, v_col0 = base // hw, (base + b_width) // hw, (base + 2 * b_width) // hw
    kernel = functools.partial(_attn_b_kernel, bq=bq, bk=bk, nkv=seq_len // bk)
    grid_spec = pltpu.PrefetchScalarGridSpec(
        num_scalar_prefetch=2,
        grid=(batch, B_HEADS, qblocks),
        in_specs=[pl.BlockSpec((bq, hw), lambda b, h, i, *_: (b * qblocks + i, q_col0 + h)),
                  pl.BlockSpec((seq_len, hw), lambda b, h, i, *_: (b, k_col0 + h),
                               pipeline_mode=pl.Buffered(1)),
                  pl.BlockSpec((seq_len, hw), lambda b, h, i, *_: (b, v_col0 + h),
                               pipeline_mode=pl.Buffered(1)),
                  pl.BlockSpec((1, hw), lambda b, h, i, *_: (0, 0))],
        out_specs=pl.BlockSpec((bq, hw), lambda b, h, i, *_: (b * qblocks + i, h)),
        scratch_shapes=[pltpu.VMEM((2 * bq, 1), F32),
                        pltpu.VMEM((2 * bq, 1), F32),
                        pltpu.VMEM((2 * bq, hw), F32)],
    )
    return pl.pallas_call(
        kernel,
        grid_spec=grid_spec,
        out_shape=jax.ShapeDtypeStruct((batch * seq_len, b_width), BF16),
        compiler_params=_cparams(("parallel", "parallel", "parallel")),
        name="attn_b",
    )(slopes, lam, proj, proj, proj, subln)


def _lane_partial_sum(p):
    ps = p[:, 0:LANES]
    for t in range(1, p.shape[1] // LANES):
        ps = ps + p[:, t * LANES:(t + 1) * LANES]
    return ps


def _attn_a_fast_kernel(q_ref, k_ref, v_ref, o_ref, q4_ref, l_ref, acc_ref, *, group, bq, bk, nkv):
    for h in range(group):
        q4_ref[h * bq:(h + 1) * bq, :] = q_ref[:, h * HEAD_DIM:(h + 1) * HEAD_DIM]
    l_ref[...] = jnp.zeros(l_ref.shape, F32)
    acc_ref[...] = jnp.zeros(acc_ref.shape, F32)

    def body(j, carry):
        off = pl.multiple_of(j * bk, bk)
        k = k_ref[pl.ds(off, bk), :]
        v = v_ref[pl.ds(off, bk), :]
        p = jnp.exp(lax.dot_general(q4_ref[...], k, _NT, preferred_element_type=F32))
        l_ref[...] += _lane_partial_sum(p)
        acc_ref[...] += jnp.dot(p.astype(BF16), v, preferred_element_type=F32)
        return carry

    lax.fori_loop(0, nkv, body, 0)
    o = acc_ref[...] / jnp.sum(l_ref[...], axis=-1, keepdims=True)
    for h in range(group):
        o_ref[:, h * HEAD_DIM:(h + 1) * HEAD_DIM] = o[h * bq:(h + 1) * bq, :].astype(o_ref.dtype)


def _attn_a_fast(proj, batch, seq_len):
    group = A_HEADS // A_KV_HEADS
    a_width = A_HEADS * HEAD_DIM
    kv_width = A_KV_HEADS * HEAD_DIM
    bq = min(FAST_A_BQ, seq_len)
    bk = min(FAST_A_BK, seq_len)
    qblocks = seq_len // bq
    gw = group * HEAD_DIM
    k_col0 = a_width // HEAD_DIM
    v_col0 = (a_width + kv_width) // HEAD_DIM
    rows = group * bq
    kernel = functools.partial(_attn_a_fast_kernel, group=group, bq=bq, bk=bk, nkv=seq_len // bk)
    return pl.pallas_call(
        kernel,
        grid=(batch, A_KV_HEADS, qblocks),
        in_specs=[pl.BlockSpec((bq, gw), lambda b, g, i: (b * qblocks + i, g)),
                  pl.BlockSpec((seq_len, HEAD_DIM), lambda b, g, i: (b, k_col0 + g),
                               pipeline_mode=pl.Buffered(1)),
                  pl.BlockSpec((seq_len, HEAD_DIM), lambda b, g, i: (b, v_col0 + g),
                               pipeline_mode=pl.Buffered(1))],
        out_specs=pl.BlockSpec((bq, gw), lambda b, g, i: (b * qblocks + i, g)),
        out_shape=jax.ShapeDtypeStruct((batch * seq_len, a_width), BF16),
        scratch_shapes=[pltpu.VMEM((rows, HEAD_DIM), BF16),
                        pltpu.VMEM((rows, LANES), F32),
                        pltpu.VMEM((rows, HEAD_DIM), F32)],
        compiler_params=_cparams(("parallel", "parallel", "parallel")),
        name="attn_a_fast",
    )(proj, proj, proj)


def _attn_b_fast_kernel(slope_ref, lam_ref, reach_ref, q_ref, k_ref, v_ref, subln_ref, o_ref, l_ref, acc_ref,
                        p_ref, *, bq, bk, nkv):
    h = pl.program_id(1)
    i = pl.program_id(2)
    slope = slope_ref[h]
    lam = lam_ref[0]
    reach = reach_ref[h]
    q = q_ref[...]
    q1 = q[:, :HEAD_DIM]
    q2 = q[:, HEAD_DIM:]
    i0 = i * bq
    qpos = (i0 + lax.broadcasted_iota(I32, (bq, 1), 0)).astype(F32)
    l_ref[...] = jnp.zeros(l_ref.shape, F32)
    acc_ref[...] = jnp.zeros(acc_ref.shape, F32)
    jb_lo = jnp.maximum(i0 - reach, 0) // bk
    jb_hi = jnp.minimum((i0 + bq - 1 + reach) // bk + 1, nkv)

    odd = (jb_hi - jb_lo) % 2
    grow_hi = jnp.logical_and(odd == 1, jb_hi < nkv).astype(I32)
    jb_hi = jb_hi + grow_hi
    jb_lo = jb_lo - (odd - grow_hi)
    npair = (jb_hi - jb_lo) // 2

    def probs(j, slot):
        off = pl.multiple_of(j * bk, bk)
        k = k_ref[pl.ds(off, bk), :]
        kpos = (j * bk + lax.broadcasted_iota(I32, (1, bk), 1)).astype(F32)
        bias = -slope * jnp.abs(qpos - kpos)
        p1 = jnp.exp(lax.dot_general(q1, k[:, :HEAD_DIM], _NT, preferred_element_type=F32) + bias)
        p2 = jnp.exp(lax.dot_general(q2, k[:, HEAD_DIM:], _NT, preferred_element_type=F32) + bias)
        l_ref[:bq, :] += _lane_partial_sum(p1)
        l_ref[bq:, :] += _lane_partial_sum(p2)
        p_ref[slot, :bq, :] = p1.astype(BF16)
        p_ref[slot, bq:, :] = p2.astype(BF16)

    def weighted_values(j, slot):
        off = pl.multiple_of(j * bk, bk)
        acc_ref[...] += jnp.dot(p_ref[slot], v_ref[pl.ds(off, bk), :], preferred_element_type=F32)

    probs(jb_lo, 0)

    def body(t, carry):
        j = jb_lo + 2 * t
        weighted_values(j, 0)
        probs(j + 1, 1)
        weighted_values(j + 1, 1)
        probs(j + 2, 0)
        return carry

    lax.fori_loop(0, npair - 1, body, 0)
    weighted_values(jb_hi - 2, 0)
    probs(jb_hi - 1, 1)
    weighted_values(jb_hi - 1, 1)

    o = acc_ref[...] / jnp.sum(l_ref[...], axis=-1, keepdims=True)
    o = o[:bq, :] - lam * o[bq:, :]
    r = lax.rsqrt(jnp.mean(o * o, axis=-1, keepdims=True) + RMS_EPS)
    o_ref[...] = ((o * r * subln_ref[...]) * (1.0 - LAM_INIT)).astype(o_ref.dtype)


def _attn_b_fast(proj, slopes, lam, reach, subln, batch, seq_len):
    a_width = A_HEADS * HEAD_DIM
    kv_width = A_KV_HEADS * HEAD_DIM
    b_width = B_HEADS * 2 * HEAD_DIM
    hw = 2 * HEAD_DIM
    bq = min(FAST_B_BQ, seq_len)
    bk = min(FAST_B_BK, seq_len // 2)
    qblocks = seq_len // bq
    base = a_width + 2 * kv_width
    assert base % hw == 0 and (seq_len // bk) % 2 == 0
    q_col0, k_col0, v_col0 = base // hw, (base + b_width) // hw, (base + 2 * b_width) // hw
    kernel = functools.partial(_attn_b_fast_kernel, bq=bq, bk=bk, nkv=seq_len // bk)
    grid_spec = pltpu.PrefetchScalarGridSpec(
        num_scalar_prefetch=3,
        grid=(batch, B_HEADS, qblocks),
        in_specs=[pl.BlockSpec((bq, hw), lambda b, h, i, *_: (b * qblocks + i, q_col0 + h)),
                  pl.BlockSpec((seq_len, hw), lambda b, h, i, *_: (b, k_col0 + h),
                               pipeline_mode=pl.Buffered(1)),
                  pl.BlockSpec((seq_len, hw), lambda b, h, i, *_: (b, v_col0 + h),
                               pipeline_mode=pl.Buffered(1)),
                  pl.BlockSpec((1, hw), lambda b, h, i, *_: (0, 0))],
        out_specs=pl.BlockSpec((bq, hw), lambda b, h, i, *_: (b * qblocks + i, h)),
        scratch_shapes=[pltpu.VMEM((2 * bq, LANES), F32),
                        pltpu.VMEM((2 * bq, hw), F32),
                        pltpu.VMEM((2, 2 * bq, bk), BF16)],
    )
    return pl.pallas_call(
        kernel,
        grid_spec=grid_spec,
        out_shape=jax.ShapeDtypeStruct((batch * seq_len, b_width), BF16),
        compiler_params=_cparams(("parallel", "parallel", "parallel")),
        name="attn_b_fast",
    )(slopes, lam, reach, proj, proj, proj, subln)


def _attention(proj, c_a, c_b, slopes, lam, subln, batch, seq_len):
    oa = lax.cond(c_a <= SAFE_LOGIT,
                  lambda p: _attn_a_fast(p, batch, seq_len),
                  lambda p: _attn_a(p, batch, seq_len), proj)
    reach = jnp.clip(jnp.ceil((c_b - EXP_ZERO_BELOW) / slopes), 0, seq_len).astype(I32)
    ob = lax.cond(c_b <= SAFE_LOGIT,
                  lambda p: _attn_b_fast(p, slopes, lam, reach, subln, batch, seq_len),
                  lambda p: _attn_b(p, slopes, lam, subln, batch, seq_len), proj)
    return oa, ob


def _outmm_kernel(a1_ref, a2_ref, w1_ref, w2_ref, o_ref):
    o_ref[...] = (jnp.dot(a1_ref[...], w1_ref[...], preferred_element_type=F32)
                  + jnp.dot(a2_ref[...], w2_ref[...], preferred_element_type=F32))


def _outmm(oa, ob, w_out):
    n, ka = oa.shape
    kb = ob.shape[1]
    d = w_out.shape[1]
    bm = min(OUTMM_BM, n)
    bn = min(OUTMM_BN, d)
    assert ka == kb and ka % 16 == 0
    return pl.pallas_call(
        _outmm_kernel,
        grid=(n // bm, d // bn),
        in_specs=[pl.BlockSpec((bm, ka), lambda i, j: (i, 0)),
                  pl.BlockSpec((bm, kb), lambda i, j: (i, 0)),
                  pl.BlockSpec((ka, bn), lambda i, j: (0, j)),
                  pl.BlockSpec((kb, bn), lambda i, j: (1, j))],
        out_specs=pl.BlockSpec((bm, bn), lambda i, j: (i, j)),
        out_shape=jax.ShapeDtypeStruct((n, d), F32),
        compiler_params=_cparams(("parallel", "parallel")),
        name="outmm",
    )(oa, ob, w_out, w_out)


def _post_body(mix_ref, x_ref, gt_ref, sc_ref, sh_ref, gpost_ref, gpre_ref, wr_ref, br_ref,
               x1_ref, h2_ref, lg_ref):
    mix = mix_ref[...]
    r = lax.rsqrt(jnp.mean(mix * mix, axis=-1, keepdims=True) + RMS_EPS)
    x1 = x_ref[...] + gt_ref[...] * (mix * r * gpost_ref[...])
    x1_ref[...] = x1
    r2 = lax.rsqrt(jnp.mean(x1 * x1, axis=-1, keepdims=True) + RMS_EPS)
    h = (x1 * r2 * gpre_ref[...]) * (1.0 + sc_ref[...]) + sh_ref[...]
    hb = h.astype(BF16)
    half = h.shape[1] // 2
    hr = hb.astype(F32)
    lo = lax.bitcast_convert_type(hr[:, :half], U32) >> 16
    hi = lax.bitcast_convert_type(hr[:, half:], U32) & jnp.uint32(0xFFFF0000)
    h2_ref[...] = hi | lo
    lg_ref[...] = jnp.dot(hb, wr_ref[...], preferred_element_type=F32) + br_ref[...]


def _post_kernel(mixp_ref, mixs_ref, xp_ref, xs_ref, gt_ref, sc_ref, sh_ref, gpost_ref, gpre_ref,
                 wr_ref, br_ref, x1_ref, h2_ref, lg_ref, *, nbp):
    i = pl.program_id(0)
    rest = (gt_ref, sc_ref, sh_ref, gpost_ref, gpre_ref, wr_ref, br_ref, x1_ref, h2_ref, lg_ref)

    @pl.when(i < nbp)
    def _():
        _post_body(mixp_ref, xp_ref, *rest)

    @pl.when(i >= nbp)
    def _():
        _post_body(mixs_ref, xs_ref, *rest)


def _post(mix_p, mix_s, x_p, x_s, mod4, g_post, g_pre, w_router, b_router, *, seq_s, nseq_p):
    n_p, d = x_p.shape
    n_s = x_s.shape[0]
    bm = min(POST_BM, seq_s)
    assert n_p % bm == 0 and n_s % bm == 0 and seq_s % bm == 0
    nbp, nbs = n_p // bm, n_s // bm
    bps = seq_s // bm
    nt = n_p + n_s
    seq_of = lambda i: jnp.where(i < nbp, 0, nseq_p + (i - nbp) // bps)
    p_idx = lambda i: (jnp.minimum(i, nbp - 1), 0)
    s_idx = lambda i: (jnp.maximum(i - nbp, 0), 0)
    mod_spec = lambda which: pl.BlockSpec((None, None, 1, d), lambda i: (seq_of(i), which, 0, 0))
    row_spec = lambda c: pl.BlockSpec((bm, c), lambda i: (i, 0))
    return pl.pallas_call(
        functools.partial(_post_kernel, nbp=nbp),
        grid=(nbp + nbs,),
        in_specs=[pl.BlockSpec((bm, d), p_idx), pl.BlockSpec((bm, d), s_idx),
                  pl.BlockSpec((bm, d), p_idx), pl.BlockSpec((bm, d), s_idx),
                  mod_spec(2), mod_spec(4), mod_spec(3),
                  pl.BlockSpec((1, d), lambda i: (0, 0)),
                  pl.BlockSpec((1, d), lambda i: (0, 0)),
                  pl.BlockSpec((d, LANES), lambda i: (0, 0)),
                  pl.BlockSpec((1, LANES), lambda i: (0, 0))],
        out_specs=[row_spec(d), row_spec(d // 2), row_spec(LANES)],
        out_shape=[jax.ShapeDtypeStruct((nt, d), F32),
                   jax.ShapeDtypeStruct((nt, d // 2), U32),
                   jax.ShapeDtypeStruct((nt, LANES), F32)],
        compiler_params=_cparams(("parallel",)),
        name="post",
    )(mix_p, mix_s, x_p, x_s, mod4, mod4, mod4, g_post, g_pre, w_router, b_router)


def _route_kernel(lg_ref, idx_ref, w_ref, rank_ref, cnt_ref, carry_ref):
    i = pl.program_id(0)

    @pl.when(i == 0)
    def _():
        carry_ref[...] = jnp.zeros(carry_ref.shape, F32)

    lg = lg_ref[...]
    bt = lg.shape[0]
    lane = lax.broadcasted_iota(I32, lg.shape, 1)
    work = lg
    vals, idxs = [], []
    onehot = jnp.zeros(lg.shape, F32)
    for _ in range(TOP_K):
        mx = jnp.max(work, axis=-1, keepdims=True)
        ix = jnp.min(jnp.where(work == mx, lane, LANES), axis=-1, keepdims=True)
        sel = lane == ix
        vals.append(mx)
        idxs.append(ix)
        work = jnp.where(sel, -jnp.inf, work)
        onehot = onehot + sel.astype(F32)
    exps = [jnp.exp(v - vals[0]) for v in vals]
    denom = exps[0]
    for e in exps[1:]:
        denom = denom + e
    tri = (lax.broadcasted_iota(I32, (bt, bt), 0) > lax.broadcasted_iota(I32, (bt, bt), 1)).astype(BF16)
    prefix = jnp.dot(tri, onehot.astype(BF16), preferred_element_type=F32) + carry_ref[...]
    idx_out = jnp.zeros(lg.shape, I32)
    w_out = jnp.zeros(lg.shape, F32)
    rank_out = jnp.zeros(lg.shape, I32)
    for k in range(TOP_K):
        rk = jnp.sum(jnp.where(lane == idxs[k], prefix, 0.0), axis=-1, keepdims=True)
        idx_out = jnp.where(lane == k, idxs[k], idx_out)
        w_out = jnp.where(lane == k, exps[k] / denom, w_out)
        rank_out = jnp.where(lane == k, rk.astype(I32), rank_out)
    idx_ref[...] = idx_out
    w_ref[...] = w_out
    rank_ref[...] = rank_out
    carry_ref[...] = carry_ref[...] + jnp.sum(onehot, axis=0, keepdims=True)
    cnt_ref[...] = carry_ref[...]


def _route(logits):
    nt = logits.shape[0]
    bt = min(ROUTE_BT, nt)
    spec = pl.BlockSpec((bt, LANES), lambda i: (i, 0))
    return pl.pallas_call(
        _route_kernel,
        grid=(nt // bt,),
        in_specs=[spec],
        out_specs=[spec, spec, spec, pl.BlockSpec((1, LANES), lambda i: (0, 0))],
        out_shape=[jax.ShapeDtypeStruct((nt, LANES), I32),
                   jax.ShapeDtypeStruct((nt, LANES), F32),
                   jax.ShapeDtypeStruct((nt, LANES), I32),
                   jax.ShapeDtypeStruct((1, LANES), F32)],
        scratch_shapes=[pltpu.VMEM((1, LANES), F32)],
        compiler_params=_cparams(("arbitrary",)),
        name="route",
    )(logits)


def _dispatch_kernel(pad_ref, dest_ref, h_ref, xs_ref, zero_ref, sem, *, bt, n_experts, total):
    i = pl.program_id(0)

    def row_copy(src, r, d):
        return pltpu.make_async_copy(src.at[pl.ds(r, 1), :], xs_ref.at[pl.ds(d, 1), :], sem)

    @pl.when(i == 0)
    def _():
        zero_ref[...] = jnp.zeros(zero_ref.shape, zero_ref.dtype)

        def fill(lo, hi):
            def start(r, c):
                row_copy(zero_ref, 0, r).start()
                return c

            def wait(r, c):
                row_copy(zero_ref, 0, r).wait()
                return c

            lax.fori_loop(lo, hi, start, 0)
            lax.fori_loop(lo, hi, wait, 0)

        def per_expert(e, c):
            fill(pad_ref[0, e], pad_ref[1, e])
            return c

        lax.fori_loop(0, n_experts, per_expert, 0)
        fill(pad_ref[1, n_experts - 1], total)

    def start(t, c):
        for k in range(TOP_K):
            row_copy(h_ref, t, dest_ref[0, 0, t * TOP_K + k]).start(priority=k % 2)
        return c

    def wait(t, c):
        for k in range(TOP_K):
            row_copy(h_ref, t, dest_ref[0, 0, t * TOP_K + k]).wait()
        return c

    lax.fori_loop(0, bt, start, 0)
    lax.fori_loop(0, bt, wait, 0)


def _dispatch(pad_info, dest, h2p, total):
    nt, half = h2p.shape
    bt = min(DISPATCH_BT, nt)
    dest3 = dest.reshape(nt // bt, 1, bt * TOP_K)
    kernel = functools.partial(_dispatch_kernel, bt=bt, n_experts=N_EXPERTS, total=total)
    grid_spec = pltpu.PrefetchScalarGridSpec(
        num_scalar_prefetch=1,
        grid=(nt // bt,),
        in_specs=[pl.BlockSpec((1, 1, bt * TOP_K), lambda i, *_: (i, 0, 0), memory_space=pltpu.SMEM),
                  pl.BlockSpec((bt, half), lambda i, *_: (i, 0))],
        out_specs=pl.BlockSpec(memory_space=pl.ANY),
        scratch_shapes=[pltpu.VMEM((8, half), U32), pltpu.SemaphoreType.DMA(())],
    )
    return pl.pallas_call(
        kernel,
        grid_spec=grid_spec,
        out_shape=jax.ShapeDtypeStruct((total, half), U32),
        compiler_params=_cparams(("arbitrary",)),
        name="dispatch",
    )(pad_info, dest3, h2p)


def _unpack_rows(p):
    lo = lax.bitcast_convert_type(p << 16, F32)
    hi = lax.bitcast_convert_type(p & jnp.uint32(0xFFFF0000), F32)
    return lo.astype(BF16), hi.astype(BF16)


def _when_rows_valid(valid, bm, run):
    half_rows = bm // 2

    @pl.when(valid > half_rows)
    def _():
        run(bm)

    @pl.when(jnp.logical_and(valid > 0, valid <= half_rows))
    def _():
        run(half_rows)


def _gu_kernel(be_ref, bv_ref, xs_ref, wg_ref, wu_ref, bg_ref, bu_ref, o_ref):
    def run(rows):
        lo, hi = _unpack_rows(xs_ref[:rows, :])
        half = lo.shape[1]

        def proj(w_ref, b_ref):
            return (jnp.dot(lo, w_ref[:half, :].astype(BF16), preferred_element_type=F32)
                    + jnp.dot(hi, w_ref[half:, :].astype(BF16), preferred_element_type=F32) + b_ref[...])

        g = jnp.minimum(proj(wg_ref, bg_ref), SWIGLU_LIMIT)
        u = jnp.clip(proj(wu_ref, bu_ref), -SWIGLU_LIMIT, SWIGLU_LIMIT)
        o_ref[:rows, :] = ((u + 1.0) * (g * jax.nn.sigmoid(SWIGLU_ALPHA * g))).astype(o_ref.dtype)

    _when_rows_valid(bv_ref[pl.program_id(1)], xs_ref.shape[0], run)


def _gu(block_e, block_valid, xs, w_gu, b_gu):
    total, half = xs.shape
    d = 2 * half
    n_e, _, two_f = w_gu.shape
    f = two_f // 2
    bm = MOE_BM
    tf = min(GU_TF, f)
    nft = f // tf
    grid_spec = pltpu.PrefetchScalarGridSpec(
        num_scalar_prefetch=2,
        grid=(nft, total // bm),
        in_specs=[pl.BlockSpec((bm, half), lambda j, i, be, bv: (i, 0)),
                  pl.BlockSpec((None, d, tf), lambda j, i, be, bv: (be[i], 0, j)),
                  pl.BlockSpec((None, d, tf), lambda j, i, be, bv: (be[i], 0, nft + j)),
                  pl.BlockSpec((None, 1, tf), lambda j, i, be, bv: (be[i], 0, j)),
                  pl.BlockSpec((None, 1, tf), lambda j, i, be, bv: (be[i], 0, nft + j))],
        out_specs=pl.BlockSpec((bm, tf), lambda j, i, be, bv: (i, j)),
    )
    return pl.pallas_call(
        _gu_kernel,
        grid_spec=grid_spec,
        out_shape=jax.ShapeDtypeStruct((total, f), BF16),
        compiler_params=_cparams(("parallel", "parallel")),
        name="moe_gu",
    )(block_e, block_valid, xs, w_gu, w_gu, b_gu.reshape(n_e, 1, two_f), b_gu.reshape(n_e, 1, two_f))


def _down_kernel(be_ref, bv_ref, a_ref, w_ref, b_ref, o_ref):
    def run(rows):
        o_ref[:rows, :] = (jnp.dot(a_ref[:rows, :], w_ref[...].astype(BF16), preferred_element_type=F32)
                           + b_ref[...])

    _when_rows_valid(bv_ref[pl.program_id(1)], a_ref.shape[0], run)


def _down(block_e, block_valid, act, w_down, b_down):
    total, f = act.shape
    n_e, _, d = w_down.shape
    bm = MOE_BM
    bn = min(DOWN_BN, d)
    grid_spec = pltpu.PrefetchScalarGridSpec(
        num_scalar_prefetch=2,
        grid=(d // bn, total // bm),
        in_specs=[pl.BlockSpec((bm, f), lambda j, i, be, bv: (i, 0)),
                  pl.BlockSpec((None, f, bn), lambda j, i, be, bv: (be[i], 0, j)),
                  pl.BlockSpec((None, 1, bn), lambda j, i, be, bv: (be[i], 0, j))],
        out_specs=pl.BlockSpec((bm, bn), lambda j, i, be, bv: (i, j)),
    )
    return pl.pallas_call(
        _down_kernel,
        grid_spec=grid_spec,
        out_shape=jax.ShapeDtypeStruct((total, d), F32),
        compiler_params=_cparams(("parallel", "parallel")),
        name="moe_down",
    )(block_e, block_valid, act, w_down, b_down.reshape(n_e, 1, d))


def _final_kernel(dest_ref, w_ref, x1_ref, gt_ref, g_ref, eo_ref, y_ref, rows_ref, sem, *, bt):
    def row_copy(t, k):
        d = dest_ref[0, 0, t * TOP_K + k]
        return pltpu.make_async_copy(eo_ref.at[pl.ds(d, 1), :], rows_ref.at[k, pl.ds(t, 1), :], sem)

    def start(t, c):
        for k in range(TOP_K):
            row_copy(t, k).start()
        return c

    def wait(t, c):
        for k in range(TOP_K):
            row_copy(t, k).wait()
        return c

    lax.fori_loop(0, bt, start, 0)
    lax.fori_loop(0, bt, wait, 0)
    w = w_ref[...]
    f = w[:, 0:1] * rows_ref[0]
    for k in range(1, TOP_K):
        f = f + w[:, k:k + 1] * rows_ref[k]
    r = lax.rsqrt(jnp.mean(f * f, axis=-1, keepdims=True) + RMS_EPS)
    y_ref[...] = x1_ref[...] + gt_ref[...] * (f * r * g_ref[...])


def _final(dest, topw, x1_all, mod4, g_post, eo, *, row0, n, seq0, seq_len):
    d = x1_all.shape[1]
    bt = min(FINAL_BT, seq_len)
    assert row0 % bt == 0 and n % bt == 0 and seq_len % bt == 0
    b0 = row0 // bt
    bps = seq_len // bt
    nt = dest.shape[0]
    dest3 = dest.reshape(nt // bt, 1, bt * TOP_K)
    return pl.pallas_call(
        functools.partial(_final_kernel, bt=bt),
        grid=(n // bt,),
        in_specs=[pl.BlockSpec((1, 1, bt * TOP_K), lambda i: (b0 + i, 0, 0), memory_space=pltpu.SMEM),
                  pl.BlockSpec((bt, LANES), lambda i: (b0 + i, 0)),
                  pl.BlockSpec((bt, d), lambda i: (b0 + i, 0)),
                  pl.BlockSpec((None, None, 1, d), lambda i: (seq0 + i // bps, 5, 0, 0)),
                  pl.BlockSpec((1, d), lambda i: (0, 0)),
                  pl.BlockSpec(memory_space=pl.ANY)],
        out_specs=pl.BlockSpec((bt, d), lambda i: (i, 0)),
        out_shape=jax.ShapeDtypeStruct((n, d), F32),
        scratch_shapes=[pltpu.VMEM((TOP_K, bt, d), F32), pltpu.SemaphoreType.DMA(())],
        compiler_params=_cparams(("arbitrary",)),
        name="final",
    )(dest3, topw, x1_all, mod4, g_post, eo)


def kernel(x_prompt, x_sample, c_prompt, c_sample, w_ada, b_ada, g_pre_mix, g_post_mix, g_pre_ffn,
           g_post_ffn, w_in, w_out, q_norm, k_norm, lam_params, subln, w_router, b_router,
           w_gate_up, b_gate_up, w_down, b_down):
    bp, sp, d = x_prompt.shape
    bs, ss, _ = x_sample.shape
    assert bp == 1, "prompt group is a single sequence"
    n_p, n_s = bp * sp, bs * ss
    nt = n_p + n_s
    l = 0
    scale = HEAD_DIM ** -0.5

    w_in_b = w_in[l].astype(BF16)
    w_out_b = w_out[l].astype(BF16)
    w_r_b = jnp.zeros((d, LANES), BF16).at[:, :N_EXPERTS].set(w_router[l].astype(BF16))
    b_r = jnp.full((1, LANES), NEG_BIG, F32).at[0, :N_EXPERTS].set(b_router[l].astype(F32))
    row = lambda g: g[l].astype(F32).reshape(1, -1)
    lp = lam_params[l].astype(F32)
    lam = (jnp.exp(jnp.sum(lp[0] * lp[1])) - jnp.exp(jnp.sum(lp[2] * lp[3])) + LAM_INIT).reshape(1)
    slopes = 2.0 ** (-8.0 * jnp.arange(1, B_HEADS + 1, dtype=F32) / B_HEADS)

    n_seq = bp + bs
    c_pad = jnp.zeros((16, d), F32).at[:n_seq].set(jnp.concatenate([c_prompt, c_sample], axis=0).astype(F32))
    mod = _ada(c_pad, w_ada[l], b_ada[l].astype(F32))
    mod4 = mod.reshape(16, N_MOD, 1, d)

    xp2 = x_prompt.reshape(n_p, d)
    xs2 = x_sample.reshape(n_s, d)
    groups = ((xp2, bp, sp, 0), (xs2, bs, ss, bp))
    mixes = []
    for x2, batch, seq_len, seq0 in groups:
        tables = (*_rope_tables(seq_len, q_norm[l], scale), *_rope_tables(seq_len, k_norm[l], 1.0))
        proj, c_a, c_b = _inproj(x2, mod4, row(g_pre_mix), w_in_b, tables, seq0=seq0, seq_len=seq_len)
        oa, ob = _attention(proj, c_a, c_b, slopes, lam, row(subln), batch, seq_len)
        mixes.append(_outmm(oa, ob, w_out_b))

    x1_all, h2p, logits = _post(mixes[0], mixes[1], xp2, xs2, mod4, row(g_post_mix), row(g_pre_ffn),
                                w_r_b, b_r, seq_s=ss, nseq_p=bp)

    idx, topw, rank, cnt = _route(logits)
    counts = cnt[0, :N_EXPERTS].astype(I32)
    padded = ((counts + MOE_BM - 1) // MOE_BM) * MOE_BM
    ends = jnp.cumsum(padded)
    pad_start = ends - padded
    dest = (pad_start[idx[:, :TOP_K]] + rank[:, :TOP_K]).astype(I32)
    n_blocks = (nt * TOP_K) // MOE_BM + N_EXPERTS
    total = n_blocks * MOE_BM
    block_starts = jnp.arange(n_blocks, dtype=I32) * MOE_BM
    block_e = jnp.minimum(jnp.sum(block_starts[:, None] >= ends[None, :], axis=1), N_EXPERTS - 1).astype(I32)
    pad_info = jnp.stack([pad_start + counts, ends]).astype(I32)

    xs = _dispatch(pad_info, dest, h2p, total)
    block_valid = jnp.clip((pad_start + counts)[block_e] - block_starts, 0, MOE_BM).astype(I32)
    assert w_gate_up.shape[0] == 1 and w_down.shape[0] == 1, "single-layer stack"
    act = _gu(block_e, block_valid, xs, w_gate_up.reshape(w_gate_up.shape[1:]), b_gate_up[l].astype(F32))
    eo = _down(block_e, block_valid, act, w_down.reshape(w_down.shape[1:]), b_down[l].astype(F32))

    y_p = _final(dest, topw, x1_all, mod4, row(g_post_ffn), eo, row0=0, n=n_p, seq0=0, seq_len=sp)
    y_s = _final(dest, topw, x1_all, mod4, row(g_post_ffn), eo, row0=n_p, n=n_s, seq0=bp, seq_len=ss)
    return y_p.reshape(bp, sp, d), y_s.reshape(bs, ss, d)
```

```python
import functools
import math

import jax
import jax.numpy as jnp
import numpy as np
from jax import lax
from jax.experimental import pallas as pl
from jax.experimental.pallas import tpu as pltpu

F32 = jnp.float32
BF16 = jnp.bfloat16
I32 = jnp.int32
U32 = jnp.uint32

HEAD_DIM = 128
GRID_W = 64
ROPE_THETA = 10000.0
RMS_EPS = 1e-6
A_HEADS = 16
A_KV_HEADS = 4
B_HEADS = 8
N_EXPERTS = 32
TOP_K = 4
SWIGLU_LIMIT = 7.0
SWIGLU_ALPHA = 1.702
N_MOD = 6
LAM_INIT = 0.8 - 0.6 * math.exp(-0.3 * 0)

LANES = 128
V7X_VMEM_BYTES = 64 * 1024 * 1024
VMEM_LIMIT = 56 * 1024 * 1024
NEG_BIG = -1e30

INPROJ_BM = 512
ATTN_A_BQ = 256
ATTN_A_BK = 512
ATTN_B_BQ = 512
ATTN_B_BK = 512
FAST_A_BQ = 256
FAST_A_BK = 2048
FAST_B_BQ = 512
FAST_B_BK = 512

SAFE_LOGIT = 40.0
EXP_ZERO_BELOW = -104.0
NORM_SLACK = 1.01
OUTMM_BM = 1024
OUTMM_BN = 512
POST_BM = 256
ROUTE_BT = 512
MOE_BM = 512
GU_TF = 512
DOWN_BN = 1024
DISPATCH_BT = 256
FINAL_BT = 256
ADA_BN = 512


def _cparams(sem):
    return pltpu.CompilerParams(dimension_semantics=("arbitrary",) * len(sem), vmem_limit_bytes=VMEM_LIMIT)


def _ada_kernel(c_ref, w_ref, b_ref, o_ref):
    c = c_ref[...]
    a = (c * jax.nn.sigmoid(c)).astype(BF16)
    o_ref[...] = jnp.dot(a, w_ref[...].astype(BF16), preferred_element_type=F32) + b_ref[...]


def _ada(c_pad, w_ada, b_ada):
    rows, d = c_pad.shape
    n = w_ada.shape[1]
    bn = min(ADA_BN, n)
    return pl.pallas_call(
        _ada_kernel,
        grid=(n // bn,),
        in_specs=[pl.BlockSpec((rows, d), lambda j: (0, 0)),
                  pl.BlockSpec((d, bn), lambda j: (0, j)),
                  pl.BlockSpec((1, bn), lambda j: (0, j))],
        out_specs=pl.BlockSpec((rows, bn), lambda j: (0, j)),
        out_shape=jax.ShapeDtypeStruct((rows, n), F32),
        compiler_params=_cparams(("parallel",)),
        name="ada",
    )(c_pad, w_ada, b_ada.reshape(1, n))


def _swap_pairs(x):
    n = x.shape[-1]
    lane = lax.broadcasted_iota(I32, x.shape, x.ndim - 1)
    up = pltpu.roll(x, n - 32, x.ndim - 1)
    dn = pltpu.roll(x, 32, x.ndim - 1)
    return jnp.where((lane % 64) < 32, up, dn)


def _inproj_kernel(x_ref, sc_ref, sh_ref, g_ref, w_ref, tqc_ref, tqs_ref, tkc_ref, tks_ref,
                   o_ref, n_ref, h_ref, *, nq, nk, qb_lo, qb_hi, kb_hi, heads_per_tile, qscale):
    j = pl.program_id(1)

    def max_sq_norm(y, best):
        n2 = jnp.max(jnp.sum(y * y, axis=-1, keepdims=True), axis=0, keepdims=True)
        return n2 if best is None else jnp.maximum(best, n2)

    @pl.when(j == 0)
    def _():
        x = x_ref[...]
        r = lax.rsqrt(jnp.mean(x * x, axis=-1, keepdims=True) + RMS_EPS)
        h = (x * r * g_ref[...]) * (1.0 + sc_ref[...]) + sh_ref[...]
        h_ref[...] = h.astype(BF16)

    acc = jnp.dot(h_ref[...], w_ref[...], preferred_element_type=F32)

    def put_norm(best):
        n_ref[...] = jnp.broadcast_to(best, n_ref.shape)

    def normed_rope(tc_ref, ts_ref):
        tc = tc_ref[...]
        ts = ts_ref[...]
        best = None
        for hh in range(heads_per_tile):
            xh = acc[:, hh * HEAD_DIM:(hh + 1) * HEAD_DIM]
            r = lax.rsqrt(jnp.mean(xh * xh, axis=-1, keepdims=True) + RMS_EPS)
            y = r * (xh * tc + _swap_pairs(xh) * ts)
            best = max_sq_norm(y, best)
            o_ref[:, hh * HEAD_DIM:(hh + 1) * HEAD_DIM] = y.astype(o_ref.dtype)
        put_norm(best)

    def scaled(factor):
        y = acc * factor
        best = None
        for hh in range(heads_per_tile):
            best = max_sq_norm(y[:, hh * HEAD_DIM:(hh + 1) * HEAD_DIM], best)
        o_ref[...] = y.astype(o_ref.dtype)
        put_norm(best)

    @pl.when(j < nq)
    def _():
        normed_rope(tqc_ref, tqs_ref)

    @pl.when(jnp.logical_and(j >= nq, j < nq + nk))
    def _():
        normed_rope(tkc_ref, tks_ref)

    @pl.when(jnp.logical_and(j >= qb_lo, j < qb_hi))
    def _():
        scaled(qscale)

    @pl.when(jnp.logical_and(j >= qb_hi, j < kb_hi))
    def _():
        scaled(1.0)

    @pl.when(jnp.logical_or(jnp.logical_and(j >= nq + nk, j < qb_lo), j >= kb_hi))
    def _():
        o_ref[...] = acc.astype(o_ref.dtype)
        n_ref[...] = jnp.zeros(n_ref.shape, F32)


def _inproj(x2d, mod4, g_pre, w_in, tables, *, seq0, seq_len):
    n, d = x2d.shape
    cols = w_in.shape[1]
    a_width = A_HEADS * HEAD_DIM
    kv_width = A_KV_HEADS * HEAD_DIM
    b_width = B_HEADS * 2 * HEAD_DIM
    bm = min(INPROJ_BM, seq_len)
    bn = min(512, kv_width)
    assert seq_len % bm == 0 and a_width % bn == 0 and kv_width % bn == 0 and b_width % bn == 0
    nq, nk = a_width // bn, kv_width // bn
    qb_lo = (a_width + 2 * kv_width) // bn
    qb_hi = qb_lo + b_width // bn
    kb_hi = qb_hi + b_width // bn
    blocks_per_seq = seq_len // bm
    seq_of = lambda i: seq0 + i // blocks_per_seq
    tab_spec = pl.BlockSpec((bm, HEAD_DIM), lambda i, j: (i % blocks_per_seq, 0))
    kernel = functools.partial(_inproj_kernel, nq=nq, nk=nk, qb_lo=qb_lo, qb_hi=qb_hi, kb_hi=kb_hi,
                               heads_per_tile=bn // HEAD_DIM, qscale=HEAD_DIM ** -0.5)
    proj, nrm = pl.pallas_call(
        kernel,
        grid=(n // bm, cols // bn),
        in_specs=[pl.BlockSpec((bm, d), lambda i, j: (i, 0)),
                  pl.BlockSpec((None, None, 1, d), lambda i, j: (seq_of(i), 1, 0, 0)),
                  pl.BlockSpec((None, None, 1, d), lambda i, j: (seq_of(i), 0, 0, 0)),
                  pl.BlockSpec((1, d), lambda i, j: (0, 0)),
                  pl.BlockSpec((d, bn), lambda i, j: (0, j)),
                  tab_spec, tab_spec, tab_spec, tab_spec],
        out_specs=[pl.BlockSpec((bm, bn), lambda i, j: (i, j)),
                   pl.BlockSpec((None, None, 8, LANES), lambda i, j: (i, j, 0, 0))],
        out_shape=[jax.ShapeDtypeStruct((n, cols), BF16),
                   jax.ShapeDtypeStruct((n // bm, cols // bn, 8, LANES), F32)],
        scratch_shapes=[pltpu.VMEM((bm, d), BF16)],
        compiler_params=_cparams(("parallel", "arbitrary")),
        name="inproj",
    )(x2d, mod4, mod4, g_pre, w_in, *tables)
    t = jnp.max(nrm[:, :, 0, 0], axis=0)
    bound = lambda qs, ks: jnp.sqrt(jnp.max(t[qs]) * jnp.max(t[ks])) * NORM_SLACK
    c_a = bound(slice(0, nq), slice(nq, nq + nk))
    c_b = bound(slice(qb_lo, qb_hi), slice(qb_hi, kb_hi))
    return proj, c_a, c_b


def _rope_tables(seq_len, gain, scale):
    half = HEAD_DIM // 2
    n_rows = seq_len // GRID_W
    inv = np.float32(ROPE_THETA) ** (-np.arange(0, half, 2, dtype=np.float32) / np.float32(half))
    ar = (np.arange(n_rows, dtype=np.float32)[:, None] * inv[None, :]).astype(np.float64)
    ac = (np.arange(GRID_W, dtype=np.float32)[:, None] * inv[None, :]).astype(np.float64)
    per_row = lambda a: jnp.broadcast_to(jnp.asarray(a, F32)[:, None, :], (n_rows, GRID_W, half // 2))
    per_col = lambda a: jnp.broadcast_to(jnp.asarray(a, F32)[None, :, :], (n_rows, GRID_W, half // 2))
    cr, sr, cc, sc = per_row(np.cos(ar)), per_row(np.sin(ar)), per_col(np.cos(ac)), per_col(np.sin(ac))
    cos = jnp.concatenate([cr, cr, cc, cc], axis=-1).reshape(seq_len, HEAD_DIM)
    sin = jnp.concatenate([-sr, sr, -sc, sc], axis=-1).reshape(seq_len, HEAD_DIM)
    g = gain.astype(F32).reshape(HEAD_DIM)
    lane = jnp.arange(HEAD_DIM)
    partner = jnp.where((lane % 64) < 32, lane + 32, lane - 32)
    return cos * (g * scale)[None, :], sin * (g[partner] * scale)[None, :]


def _online_softmax_step(s, v, m_ref, l_ref, acc_ref):
    m_prev = m_ref[...]
    m_new = jnp.maximum(m_prev, jnp.max(s, axis=-1, keepdims=True))
    alpha = jnp.exp(m_prev - m_new)
    p = jnp.exp(s - m_new)
    l_ref[...] = alpha * l_ref[...] + jnp.sum(p, axis=-1, keepdims=True)
    acc_ref[...] = alpha * acc_ref[...] + jnp.dot(p.astype(BF16), v, preferred_element_type=F32)
    m_ref[...] = m_new


_NT = (((1,), (1,)), ((), ()))


def _attn_a_kernel(q_ref, k_ref, v_ref, o_ref, q4_ref, m_ref, l_ref, acc_ref, *, group, bq, bk, nkv):
    for h in range(group):
        q4_ref[h * bq:(h + 1) * bq, :] = q_ref[:, h * HEAD_DIM:(h + 1) * HEAD_DIM]
    m_ref[...] = jnp.full(m_ref.shape, -jnp.inf, F32)
    l_ref[...] = jnp.zeros(l_ref.shape, F32)
    acc_ref[...] = jnp.zeros(acc_ref.shape, F32)

    def body(j, carry):
        off = pl.multiple_of(j * bk, bk)
        k = k_ref[pl.ds(off, bk), :]
        v = v_ref[pl.ds(off, bk), :]
        s = lax.dot_general(q4_ref[...], k, _NT, preferred_element_type=F32)
        _online_softmax_step(s, v, m_ref, l_ref, acc_ref)
        return carry

    lax.fori_loop(0, nkv, body, 0)
    o = acc_ref[...] / l_ref[...]
    for h in range(group):
        o_ref[:, h * HEAD_DIM:(h + 1) * HEAD_DIM] = o[h * bq:(h + 1) * bq, :].astype(o_ref.dtype)


def _attn_a(proj, batch, seq_len):
    group = A_HEADS // A_KV_HEADS
    a_width = A_HEADS * HEAD_DIM
    kv_width = A_KV_HEADS * HEAD_DIM
    bq = min(ATTN_A_BQ, seq_len)
    bk = min(ATTN_A_BK, seq_len)
    qblocks = seq_len // bq
    gw = group * HEAD_DIM
    k_col0 = a_width // HEAD_DIM
    v_col0 = (a_width + kv_width) // HEAD_DIM
    rows = group * bq
    kernel = functools.partial(_attn_a_kernel, group=group, bq=bq, bk=bk, nkv=seq_len // bk)
    return pl.pallas_call(
        kernel,
        grid=(batch, A_KV_HEADS, qblocks),
        in_specs=[pl.BlockSpec((bq, gw), lambda b, g, i: (b * qblocks + i, g)),
                  pl.BlockSpec((seq_len, HEAD_DIM), lambda b, g, i: (b, k_col0 + g),
                               pipeline_mode=pl.Buffered(1)),
                  pl.BlockSpec((seq_len, HEAD_DIM), lambda b, g, i: (b, v_col0 + g),
                               pipeline_mode=pl.Buffered(1))],
        out_specs=pl.BlockSpec((bq, gw), lambda b, g, i: (b * qblocks + i, g)),
        out_shape=jax.ShapeDtypeStruct((batch * seq_len, a_width), BF16),
        scratch_shapes=[pltpu.VMEM((rows, HEAD_DIM), BF16),
                        pltpu.VMEM((rows, 1), F32),
                        pltpu.VMEM((rows, 1), F32),
                        pltpu.VMEM((rows, HEAD_DIM), F32)],
        compiler_params=_cparams(("parallel", "parallel", "parallel")),
        name="attn_a",
    )(proj, proj, proj)


def _attn_b_kernel(slope_ref, lam_ref, q_ref, k_ref, v_ref, subln_ref, o_ref, m_ref, l_ref, acc_ref,
                   *, bq, bk, nkv):
    h = pl.program_id(1)
    i = pl.program_id(2)
    slope = slope_ref[h]
    lam = lam_ref[0]
    q = q_ref[...]
    q1 = q[:, :HEAD_DIM]
    q2 = q[:, HEAD_DIM:]
    qpos = (i * bq + lax.broadcasted_iota(I32, (bq, 1), 0)).astype(F32)
    m_ref[...] = jnp.full(m_ref.shape, -jnp.inf, F32)
    l_ref[...] = jnp.zeros(l_ref.shape, F32)
    acc_ref[...] = jnp.zeros(acc_ref.shape, F32)

    def body(j, carry):
        off = pl.multiple_of(j * bk, bk)
        k = k_ref[pl.ds(off, bk), :]
        v = v_ref[pl.ds(off, bk), :]
        kpos = (j * bk + lax.broadcasted_iota(I32, (1, bk), 1)).astype(F32)
        bias = -slope * jnp.abs(qpos - kpos)
        s1 = lax.dot_general(q1, k[:, :HEAD_DIM], _NT, preferred_element_type=F32) + bias
        s2 = lax.dot_general(q2, k[:, HEAD_DIM:], _NT, preferred_element_type=F32) + bias
        _online_softmax_step(jnp.concatenate([s1, s2], axis=0), v, m_ref, l_ref, acc_ref)
        return carry

    lax.fori_loop(0, nkv, body, 0)
    o = acc_ref[...] / l_ref[...]
    o = o[:bq, :] - lam * o[bq:, :]
    r = lax.rsqrt(jnp.mean(o * o, axis=-1, keepdims=True) + RMS_EPS)
    o_ref[...] = ((o * r * subln_ref[...]) * (1.0 - LAM_INIT)).astype(o_ref.dtype)


def _attn_b(proj, slopes, lam, subln, batch, seq_len):
    a_width = A_HEADS * HEAD_DIM
    kv_width = A_KV_HEADS * HEAD_DIM
    b_width = B_HEADS * 2 * HEAD_DIM
    hw = 2 * HEAD_DIM
    bq = min(ATTN_B_BQ, seq_len)
    bk = min(ATTN_B_BK, seq_len)
    qblocks = seq_len // bq
    base = a_width + 2 * kv_width
    assert base % hw == 0
    q_col0, k_col0, v_col0 = base // hw, (base + b_width) // hw, (base + 2 * b_width) // hw
    kernel = functools.partial(_attn_b_kernel, bq=bq, bk=bk, nkv=seq_len // bk)
    grid_spec = pltpu.PrefetchScalarGridSpec(
        num_scalar_prefetch=2,
        grid=(batch, B_HEADS, qblocks),
        in_specs=[pl.BlockSpec((bq, hw), lambda b, h, i, *_: (b * qblocks + i, q_col0 + h)),
                  pl.BlockSpec((seq_len, hw), lambda b, h, i, *_: (b, k_col0 + h),
                               pipeline_mode=pl.Buffered(1)),
                  pl.BlockSpec((seq_len, hw), lambda b, h, i, *_: (b, v_col0 + h),
                               pipeline_mode=pl.Buffered(1)),
                  pl.BlockSpec((1, hw), lambda b, h, i, *_: (0, 0))],
        out_specs=pl.BlockSpec((bq, hw), lambda b, h, i, *_: (b * qblocks + i, h)),
        scratch_shapes=[pltpu.VMEM((2 * bq, 1), F32),
                        pltpu.VMEM((2 * bq, 1), F32),
                        pltpu.VMEM((2 * bq, hw), F32)],
    )
    return pl.pallas_call(
        kernel,
        grid_spec=grid_spec,
        out_shape=jax.ShapeDtypeStruct((batch * seq_len, b_width), BF16),
        compiler_params=_cparams(("parallel", "parallel", "parallel")),
        name="attn_b",
    )(slopes, lam, proj, proj, proj, subln)


def _lane_partial_sum(p):
    ps = p[:, 0:LANES]
    for t in range(1, p.shape[1] // LANES):
        ps = ps + p[:, t * LANES:(t + 1) * LANES]
    return ps


def _attn_a_fast_kernel(q_ref, k_ref, v_ref, o_ref, q4_ref, l_ref, acc_ref, *, group, bq, bk, nkv):
    for h in range(group):
        q4_ref[h * bq:(h + 1) * bq, :] = q_ref[:, h * HEAD_DIM:(h + 1) * HEAD_DIM]
    l_ref[...] = jnp.zeros(l_ref.shape, F32)
    acc_ref[...] = jnp.zeros(acc_ref.shape, F32)

    def body(j, carry):
        off = pl.multiple_of(j * bk, bk)
        k = k_ref[pl.ds(off, bk), :]
        v = v_ref[pl.ds(off, bk), :]
        p = jnp.exp(lax.dot_general(q4_ref[...], k, _NT, preferred_element_type=F32))
        l_ref[...] += _lane_partial_sum(p)
        acc_ref[...] += jnp.dot(p.astype(BF16), v, preferred_element_type=F32)
        return carry

    lax.fori_loop(0, nkv, body, 0)
    o = acc_ref[...] / jnp.sum(l_ref[...], axis=-1, keepdims=True)
    for h in range(group):
        o_ref[:, h * HEAD_DIM:(h + 1) * HEAD_DIM] = o[h * bq:(h + 1) * bq, :].astype(o_ref.dtype)


def _attn_a_fast(proj, batch, seq_len):
    group = A_HEADS // A_KV_HEADS
    a_width = A_HEADS * HEAD_DIM
    kv_width = A_KV_HEADS * HEAD_DIM
    bq = min(FAST_A_BQ, seq_len)
    bk = min(FAST_A_BK, seq_len)
    qblocks = seq_len // bq
    gw = group * HEAD_DIM
    k_col0 = a_width // HEAD_DIM
    v_col0 = (a_width + kv_width) // HEAD_DIM
    rows = group * bq
    kernel = functools.partial(_attn_a_fast_kernel, group=group, bq=bq, bk=bk, nkv=seq_len // bk)
    return pl.pallas_call(
        kernel,
        grid=(batch, A_KV_HEADS, qblocks),
        in_specs=[pl.BlockSpec((bq, gw), lambda b, g, i: (b * qblocks + i, g)),
                  pl.BlockSpec((seq_len, HEAD_DIM), lambda b, g, i: (b, k_col0 + g),
                               pipeline_mode=pl.Buffered(1)),
                  pl.BlockSpec((seq_len, HEAD_DIM), lambda b, g, i: (b, v_col0 + g),
                               pipeline_mode=pl.Buffered(1))],
        out_specs=pl.BlockSpec((bq, gw), lambda b, g, i: (b * qblocks + i, g)),
        out_shape=jax.ShapeDtypeStruct((batch * seq_len, a_width), BF16),
        scratch_shapes=[pltpu.VMEM((rows, HEAD_DIM), BF16),
                        pltpu.VMEM((rows, LANES), F32),
                        pltpu.VMEM((rows, HEAD_DIM), F32)],
        compiler_params=_cparams(("parallel", "parallel", "parallel")),
        name="attn_a_fast",
    )(proj, proj, proj)


def _attn_b_fast_kernel(slope_ref, lam_ref, reach_ref, q_ref, k_ref, v_ref, subln_ref, o_ref, l_ref, acc_ref,
                        p_ref, *, bq, bk, nkv):
    h = pl.program_id(1)
    i = pl.program_id(2)
    slope = slope_ref[h]
    lam = lam_ref[0]
    reach = reach_ref[h]
    q = q_ref[...]
    q1 = q[:, :HEAD_DIM]
    q2 = q[:, HEAD_DIM:]
    i0 = i * bq
    qpos = (i0 + lax.broadcasted_iota(I32, (bq, 1), 0)).astype(F32)
    l_ref[...] = jnp.zeros(l_ref.shape, F32)
    acc_ref[...] = jnp.zeros(acc_ref.shape, F32)
    jb_lo = jnp.maximum(i0 - reach, 0) // bk
    jb_hi = jnp.minimum((i0 + bq - 1 + reach) // bk + 1, nkv)

    odd = (jb_hi - jb_lo) % 2
    grow_hi = jnp.logical_and(odd == 1, jb_hi < nkv).astype(I32)
    jb_hi = jb_hi + grow_hi
    jb_lo = jb_lo - (odd - grow_hi)
    npair = (jb_hi - jb_lo) // 2

    def probs(j, slot):
        off = pl.multiple_of(j * bk, bk)
        k = k_ref[pl.ds(off, bk), :]
        kpos = (j * bk + lax.broadcasted_iota(I32, (1, bk), 1)).astype(F32)
        bias = -slope * jnp.abs(qpos - kpos)
        p1 = jnp.exp(lax.dot_general(q1, k[:, :HEAD_DIM], _NT, preferred_element_type=F32) + bias)
        p2 = jnp.exp(lax.dot_general(q2, k[:, HEAD_DIM:], _NT, preferred_element_type=F32) + bias)
        l_ref[:bq, :] += _lane_partial_sum(p1)
        l_ref[bq:, :] += _lane_partial_sum(p2)
        p_ref[slot, :bq, :] = p1.astype(BF16)
        p_ref[slot, bq:, :] = p2.astype(BF16)

    def weighted_values(j, slot):
        off = pl.multiple_of(j * bk, bk)
        acc_ref[...] += jnp.dot(p_ref[slot], v_ref[pl.ds(off, bk), :], preferred_element_type=F32)

    probs(jb_lo, 0)

    def body(t, carry):
        j = jb_lo + 2 * t
        weighted_values(j, 0)
        probs(j + 1, 1)
        weighted_values(j + 1, 1)
        probs(j + 2, 0)
        return carry

    lax.fori_loop(0, npair - 1, body, 0)
    weighted_values(jb_hi - 2, 0)
    probs(jb_hi - 1, 1)
    weighted_values(jb_hi - 1, 1)

    o = acc_ref[...] / jnp.sum(l_ref[...], axis=-1, keepdims=True)
    o = o[:bq, :] - lam * o[bq:, :]
    r = lax.rsqrt(jnp.mean(o * o, axis=-1, keepdims=True) + RMS_EPS)
    o_ref[...] = ((o * r * subln_ref[...]) * (1.0 - LAM_INIT)).astype(o_ref.dtype)


def _attn_b_fast(proj, slopes, lam, reach, subln, batch, seq_len):
    a_width = A_HEADS * HEAD_DIM
    kv_width = A_KV_HEADS * HEAD_DIM
    b_width = B_HEADS * 2 * HEAD_DIM
    hw = 2 * HEAD_DIM
    bq = min(FAST_B_BQ, seq_len)
    bk = min(FAST_B_BK, seq_len // 2)
    qblocks = seq_len // bq
    base = a_width + 2 * kv_width
    assert base % hw == 0 and (seq_len // bk) % 2 == 0
    q_col0, k_col0, v_col0 = base // hw, (base + b_width) // hw, (base + 2 * b_width) // hw
    kernel = functools.partial(_attn_b_fast_kernel, bq=bq, bk=bk, nkv=seq_len // bk)
    grid_spec = pltpu.PrefetchScalarGridSpec(
        num_scalar_prefetch=3,
        grid=(batch, B_HEADS, qblocks),
        in_specs=[pl.BlockSpec((bq, hw), lambda b, h, i, *_: (b * qblocks + i, q_col0 + h)),
                  pl.BlockSpec((seq_len, hw), lambda b, h, i, *_: (b, k_col0 + h),
                               pipeline_mode=pl.Buffered(1)),
                  pl.BlockSpec((seq_len, hw), lambda b, h, i, *_: (b, v_col0 + h),
                               pipeline_mode=pl.Buffered(1)),
                  pl.BlockSpec((1, hw), lambda b, h, i, *_: (0, 0))],
        out_specs=pl.BlockSpec((bq, hw), lambda b, h, i, *_: (b * qblocks + i, h)),
        scratch_shapes=[pltpu.VMEM((2 * bq, LANES), F32),
                        pltpu.VMEM((2 * bq, hw), F32),
                        pltpu.VMEM((2, 2 * bq, bk), BF16)],
    )
    return pl.pallas_call(
        kernel,
        grid_spec=grid_spec,
        out_shape=jax.ShapeDtypeStruct((batch * seq_len, b_width), BF16),
        compiler_params=_cparams(("parallel", "parallel", "parallel")),
        name="attn_b_fast",
    )(slopes, lam, reach, proj, proj, proj, subln)


def _attention(proj, c_a, c_b, slopes, lam, subln, batch, seq_len):
    oa = lax.cond(c_a <= SAFE_LOGIT,
                  lambda p: _attn_a_fast(p, batch, seq_len),
                  lambda p: _attn_a(p, batch, seq_len), proj)
    reach = jnp.clip(jnp.ceil((c_b - EXP_ZERO_BELOW) / slopes), 0, seq_len).astype(I32)
    ob = lax.cond(c_b <= SAFE_LOGIT,
                  lambda p: _attn_b_fast(p, slopes, lam, reach, subln, batch, seq_len),
                  lambda p: _attn_b(p, slopes, lam, subln, batch, seq_len), proj)
    return oa, ob


def _outmm_kernel(a1_ref, a2_ref, w1_ref, w2_ref, o_ref):
    o_ref[...] = (jnp.dot(a1_ref[...], w1_ref[...], preferred_element_type=F32)
                  + jnp.dot(a2_ref[...], w2_ref[...], preferred_element_type=F32))


def _outmm(oa, ob, w_out):
    n, ka = oa.shape
    kb = ob.shape[1]
    d = w_out.shape[1]
    bm = min(OUTMM_BM, n)
    bn = min(OUTMM_BN, d)
    assert ka == kb and ka % 16 == 0
    return pl.pallas_call(
        _outmm_kernel,
        grid=(n // bm, d // bn),
        in_specs=[pl.BlockSpec((bm, ka), lambda i, j: (i, 0)),
                  pl.BlockSpec((bm, kb), lambda i, j: (i, 0)),
                  pl.BlockSpec((ka, bn), lambda i, j: (0, j)),
                  pl.BlockSpec((kb, bn), lambda i, j: (1, j))],
        out_specs=pl.BlockSpec((bm, bn), lambda i, j: (i, j)),
        out_shape=jax.ShapeDtypeStruct((n, d), F32),
        compiler_params=_cparams(("parallel", "parallel")),
        name="outmm",
    )(oa, ob, w_out, w_out)


def _post_body(mix_ref, x_ref, gt_ref, sc_ref, sh_ref, gpost_ref, gpre_ref, wr_ref, br_ref,
               x1_ref, h2_ref, lg_ref):
    mix = mix_ref[...]
    r = lax.rsqrt(jnp.mean(mix * mix, axis=-1, keepdims=True) + RMS_EPS)
    x1 = x_ref[...] + gt_ref[...] * (mix * r * gpost_ref[...])
    x1_ref[...] = x1
    r2 = lax.rsqrt(jnp.mean(x1 * x1, axis=-1, keepdims=True) + RMS_EPS)
    h = (x1 * r2 * gpre_ref[...]) * (1.0 + sc_ref[...]) + sh_ref[...]
    hb = h.astype(BF16)
    half = h.shape[1] // 2
    hr = hb.astype(F32)
    lo = lax.bitcast_convert_type(hr[:, :half], U32) >> 16
    hi = lax.bitcast_convert_type(hr[:, half:], U32) & jnp.uint32(0xFFFF0000)
    h2_ref[...] = hi | lo
    lg_ref[...] = jnp.dot(hb, wr_ref[...], preferred_element_type=F32) + br_ref[...]


def _post_kernel(mixp_ref, mixs_ref, xp_ref, xs_ref, gt_ref, sc_ref, sh_ref, gpost_ref, gpre_ref,
                 wr_ref, br_ref, x1_ref, h2_ref, lg_ref, *, nbp):
    i = pl.program_id(0)
    rest = (gt_ref, sc_ref, sh_ref, gpost_ref, gpre_ref, wr_ref, br_ref, x1_ref, h2_ref, lg_ref)

    @pl.when(i < nbp)
    def _():
        _post_body(mixp_ref, xp_ref, *rest)

    @pl.when(i >= nbp)
    def _():
        _post_body(mixs_ref, xs_ref, *rest)


def _post(mix_p, mix_s, x_p, x_s, mod4, g_post, g_pre, w_router, b_router, *, seq_s, nseq_p):
    n_p, d = x_p.shape
    n_s = x_s.shape[0]
    bm = min(POST_BM, seq_s)
    assert n_p % bm == 0 and n_s % bm == 0 and seq_s % bm == 0
    nbp, nbs = n_p // bm, n_s // bm
    bps = seq_s // bm
    nt = n_p + n_s
    seq_of = lambda i: jnp.where(i < nbp, 0, nseq_p + (i - nbp) // bps)
    p_idx = lambda i: (jnp.minimum(i, nbp - 1), 0)
    s_idx = lambda i: (jnp.maximum(i - nbp, 0), 0)
    mod_spec = lambda which: pl.BlockSpec((None, None, 1, d), lambda i: (seq_of(i), which, 0, 0))
    row_spec = lambda c: pl.BlockSpec((bm, c), lambda i: (i, 0))
    return pl.pallas_call(
        functools.partial(_post_kernel, nbp=nbp),
        grid=(nbp + nbs,),
        in_specs=[pl.BlockSpec((bm, d), p_idx), pl.BlockSpec((bm, d), s_idx),
                  pl.BlockSpec((bm, d), p_idx), pl.BlockSpec((bm, d), s_idx),
                  mod_spec(2), mod_spec(4), mod_spec(3),
                  pl.BlockSpec((1, d), lambda i: (0, 0)),
                  pl.BlockSpec((1, d), lambda i: (0, 0)),
                  pl.BlockSpec((d, LANES), lambda i: (0, 0)),
                  pl.BlockSpec((1, LANES), lambda i: (0, 0))],
        out_specs=[row_spec(d), row_spec(d // 2), row_spec(LANES)],
        out_shape=[jax.ShapeDtypeStruct((nt, d), F32),
                   jax.ShapeDtypeStruct((nt, d // 2), U32),
                   jax.ShapeDtypeStruct((nt, LANES), F32)],
        compiler_params=_cparams(("parallel",)),
        name="post",
    )(mix_p, mix_s, x_p, x_s, mod4, mod4, mod4, g_post, g_pre, w_router, b_router)


def _route_kernel(lg_ref, idx_ref, w_ref, rank_ref, cnt_ref, carry_ref):
    i = pl.program_id(0)

    @pl.when(i == 0)
    def _():
        carry_ref[...] = jnp.zeros(carry_ref.shape, F32)

    lg = lg_ref[...]
    bt = lg.shape[0]
    lane = lax.broadcasted_iota(I32, lg.shape, 1)
    work = lg
    vals, idxs = [], []
    onehot = jnp.zeros(lg.shape, F32)
    for _ in range(TOP_K):
        mx = jnp.max(work, axis=-1, keepdims=True)
        ix = jnp.min(jnp.where(work == mx, lane, LANES), axis=-1, keepdims=True)
        sel = lane == ix
        vals.append(mx)
        idxs.append(ix)
        work = jnp.where(sel, -jnp.inf, work)
        onehot = onehot + sel.astype(F32)
    exps = [jnp.exp(v - vals[0]) for v in vals]
    denom = exps[0]
    for e in exps[1:]:
        denom = denom + e
    tri = (lax.broadcasted_iota(I32, (bt, bt), 0) > lax.broadcasted_iota(I32, (bt, bt), 1)).astype(BF16)
    prefix = jnp.dot(tri, onehot.astype(BF16), preferred_element_type=F32) + carry_ref[...]
    idx_out = jnp.zeros(lg.shape, I32)
    w_out = jnp.zeros(lg.shape, F32)
    rank_out = jnp.zeros(lg.shape, I32)
    for k in range(TOP_K):
        rk = jnp.sum(jnp.where(lane == idxs[k], prefix, 0.0), axis=-1, keepdims=True)
        idx_out = jnp.where(lane == k, idxs[k], idx_out)
        w_out = jnp.where(lane == k, exps[k] / denom, w_out)
        rank_out = jnp.where(lane == k, rk.astype(I32), rank_out)
    idx_ref[...] = idx_out
    w_ref[...] = w_out
    rank_ref[...] = rank_out
    carry_ref[...] = carry_ref[...] + jnp.sum(onehot, axis=0, keepdims=True)
    cnt_ref[...] = carry_ref[...]


def _route(logits):
    nt = logits.shape[0]
    bt = min(ROUTE_BT, nt)
    spec = pl.BlockSpec((bt, LANES), lambda i: (i, 0))
    return pl.pallas_call(
        _route_kernel,
        grid=(nt // bt,),
        in_specs=[spec],
        out_specs=[spec, spec, spec, pl.BlockSpec((1, LANES), lambda i: (0, 0))],
        out_shape=[jax.ShapeDtypeStruct((nt, LANES), I32),
                   jax.ShapeDtypeStruct((nt, LANES), F32),
                   jax.ShapeDtypeStruct((nt, LANES), I32),
                   jax.ShapeDtypeStruct((1, LANES), F32)],
        scratch_shapes=[pltpu.VMEM((1, LANES), F32)],
        compiler_params=_cparams(("arbitrary",)),
        name="route",
    )(logits)


def _dispatch_kernel(pad_ref, dest_ref, h_ref, xs_ref, zero_ref, sem, *, bt, n_experts, total):
    i = pl.program_id(0)

    def row_copy(src, r, d):
        return pltpu.make_async_copy(src.at[pl.ds(r, 1), :], xs_ref.at[pl.ds(d, 1), :], sem)

    @pl.when(i == 0)
    def _():
        zero_ref[...] = jnp.zeros(zero_ref.shape, zero_ref.dtype)

        def fill(lo, hi):
            def start(r, c):
                row_copy(zero_ref, 0, r).start()
                return c

            def wait(r, c):
                row_copy(zero_ref, 0, r).wait()
                return c

            lax.fori_loop(lo, hi, start, 0)
            lax.fori_loop(lo, hi, wait, 0)

        def per_expert(e, c):
            fill(pad_ref[0, e], pad_ref[1, e])
            return c

        lax.fori_loop(0, n_experts, per_expert, 0)
        fill(pad_ref[1, n_experts - 1], total)

    def start(t, c):
        for k in range(TOP_K):
            row_copy(h_ref, t, dest_ref[0, 0, t * TOP_K + k]).start(priority=k % 2)
        return c

    def wait(t, c):
        for k in range(TOP_K):
            row_copy(h_ref, t, dest_ref[0, 0, t * TOP_K + k]).wait()
        return c

    lax.fori_loop(0, bt, start, 0)
    lax.fori_loop(0, bt, wait, 0)


def _dispatch(pad_info, dest, h2p, total):
    nt, half = h2p.shape
    bt = min(DISPATCH_BT, nt)
    dest3 = dest.reshape(nt // bt, 1, bt * TOP_K)
    kernel = functools.partial(_dispatch_kernel, bt=bt, n_experts=N_EXPERTS, total=total)
    grid_spec = pltpu.PrefetchScalarGridSpec(
        num_scalar_prefetch=1,
        grid=(nt // bt,),
        in_specs=[pl.BlockSpec((1, 1, bt * TOP_K), lambda i, *_: (i, 0, 0), memory_space=pltpu.SMEM),
                  pl.BlockSpec((bt, half), lambda i, *_: (i, 0))],
        out_specs=pl.BlockSpec(memory_space=pl.ANY),
        scratch_shapes=[pltpu.VMEM((8, half), U32), pltpu.SemaphoreType.DMA(())],
    )
    return pl.pallas_call(
        kernel,
        grid_spec=grid_spec,
        out_shape=jax.ShapeDtypeStruct((total, half), U32),
        compiler_params=_cparams(("arbitrary",)),
        name="dispatch",
    )(pad_info, dest3, h2p)


def _unpack_rows(p):
    lo = lax.bitcast_convert_type(p << 16, F32)
    hi = lax.bitcast_convert_type(p & jnp.uint32(0xFFFF0000), F32)
    return lo.astype(BF16), hi.astype(BF16)


def _when_rows_valid(valid, bm, run):
    half_rows = bm // 2

    @pl.when(valid > half_rows)
    def _():
        run(bm)

    @pl.when(jnp.logical_and(valid > 0, valid <= half_rows))
    def _():
        run(half_rows)


def _segment_weights(meta, copies, n_outer):
    be, bv, first, segidx, nxt, nseg = meta
    j = pl.program_id(0)
    i = pl.program_id(1)
    g = j * nseg[0] + segidx[i]
    slot = g % 2

    @pl.when(jnp.logical_and(bv[i] > 0, first[i] == 1))
    def _():
        @pl.when(g == 0)
        def _():
            for c in copies(be[i], j, slot):
                c.start()

        for c in copies(be[i], j, slot):
            c.wait()
        last = (segidx[i] == nseg[0] - 1).astype(I32)

        @pl.when(jnp.logical_not(jnp.logical_and(last == 1, j == n_outer - 1)))
        def _():
            for c in copies(nxt[i], j + last, 1 - slot):
                c.start()

    return slot


def _gu_kernel(be, bv, first, segidx, nxt, nseg, xs_ref, w_hbm, bg_ref, bu_ref, o_ref, wbuf, sem,
               *, nft, tf, f):
    def copies(e, j, slot):
        col = pl.multiple_of(j * tf, tf)
        return [pltpu.make_async_copy(w_hbm.at[e, :, pl.ds(col, tf)], wbuf.at[slot, 0], sem.at[slot]),
                pltpu.make_async_copy(w_hbm.at[e, :, pl.ds(f + col, tf)], wbuf.at[slot, 1], sem.at[slot])]

    slot = _segment_weights((be, bv, first, segidx, nxt, nseg), copies, nft)

    def run(rows):
        lo, hi = _unpack_rows(xs_ref[:rows, :])
        half = lo.shape[1]

        def proj(which, b_ref):
            w = wbuf.at[slot, which]
            return (jnp.dot(lo, w[:half, :].astype(BF16), preferred_element_type=F32)
                    + jnp.dot(hi, w[half:, :].astype(BF16), preferred_element_type=F32) + b_ref[...])

        g = jnp.minimum(proj(0, bg_ref), SWIGLU_LIMIT)
        u = jnp.clip(proj(1, bu_ref), -SWIGLU_LIMIT, SWIGLU_LIMIT)
        o_ref[:rows, :] = ((u + 1.0) * (g * jax.nn.sigmoid(SWIGLU_ALPHA * g))).astype(o_ref.dtype)

    _when_rows_valid(bv[pl.program_id(1)], xs_ref.shape[0], run)


def _gu(meta, xs, w_gu, b_gu):
    total, half = xs.shape
    d = 2 * half
    n_e, _, two_f = w_gu.shape
    f = two_f // 2
    bm = MOE_BM
    tf = min(GU_TF, f)
    nft = f // tf
    grid_spec = pltpu.PrefetchScalarGridSpec(
        num_scalar_prefetch=6,
        grid=(nft, total // bm),
        in_specs=[pl.BlockSpec((bm, half), lambda j, i, *_: (i, 0)),
                  pl.BlockSpec(memory_space=pl.ANY),
                  pl.BlockSpec((None, 1, tf), lambda j, i, be, *_: (be[i], 0, j)),
                  pl.BlockSpec((None, 1, tf), lambda j, i, be, *_: (be[i], 0, nft + j))],
        out_specs=pl.BlockSpec((bm, tf), lambda j, i, *_: (i, j)),
        scratch_shapes=[pltpu.VMEM((2, 2, d, tf), F32), pltpu.SemaphoreType.DMA((2,))],
    )
    return pl.pallas_call(
        functools.partial(_gu_kernel, nft=nft, tf=tf, f=f),
        grid_spec=grid_spec,
        out_shape=jax.ShapeDtypeStruct((total, f), BF16),
        compiler_params=_cparams(("arbitrary", "arbitrary")),
        name="moe_gu",
    )(*meta, xs, w_gu, b_gu.reshape(n_e, 1, two_f), b_gu.reshape(n_e, 1, two_f))


def _down_kernel(be, bv, first, segidx, nxt, nseg, a_ref, w_hbm, b_ref, o_ref, wbuf, sem, *, nnt, bn):
    def copies(e, j, slot):
        col = pl.multiple_of(j * bn, bn)
        return [pltpu.make_async_copy(w_hbm.at[e, :, pl.ds(col, bn)], wbuf.at[slot], sem.at[slot])]

    slot = _segment_weights((be, bv, first, segidx, nxt, nseg), copies, nnt)

    def run(rows):
        w = wbuf[slot].astype(BF16)
        o_ref[:rows, :] = jnp.dot(a_ref[:rows, :], w, preferred_element_type=F32) + b_ref[...]

    _when_rows_valid(bv[pl.program_id(1)], a_ref.shape[0], run)


def _down(meta, act, w_down, b_down):
    total, f = act.shape
    n_e, _, d = w_down.shape
    bm = MOE_BM
    bn = min(DOWN_BN, d)
    nnt = d // bn
    grid_spec = pltpu.PrefetchScalarGridSpec(
        num_scalar_prefetch=6,
        grid=(nnt, total // bm),
        in_specs=[pl.BlockSpec((bm, f), lambda j, i, *_: (i, 0)),
                  pl.BlockSpec(memory_space=pl.ANY),
                  pl.BlockSpec((None, 1, bn), lambda j, i, be, *_: (be[i], 0, j))],
        out_specs=pl.BlockSpec((bm, bn), lambda j, i, *_: (i, j)),
        scratch_shapes=[pltpu.VMEM((2, f, bn), F32), pltpu.SemaphoreType.DMA((2,))],
    )
    return pl.pallas_call(
        functools.partial(_down_kernel, nnt=nnt, bn=bn),
        grid_spec=grid_spec,
        out_shape=jax.ShapeDtypeStruct((total, d), F32),
        compiler_params=_cparams(("arbitrary", "arbitrary")),
        name="moe_down",
    )(*meta, act, w_down, b_down.reshape(n_e, 1, d))


def _final_kernel(dest_ref, w_ref, x1_ref, gt_ref, g_ref, eo_ref, y_ref, rows_ref, sem, *, bt):
    def row_copy(t, k):
        d = dest_ref[0, 0, t * TOP_K + k]
        return pltpu.make_async_copy(eo_ref.at[pl.ds(d, 1), :], rows_ref.at[k, pl.ds(t, 1), :], sem)

    def start(t, c):
        for k in range(TOP_K):
            row_copy(t, k).start()
        return c

    def wait(t, c):
        for k in range(TOP_K):
            row_copy(t, k).wait()
        return c

    lax.fori_loop(0, bt, start, 0)
    lax.fori_loop(0, bt, wait, 0)
    w = w_ref[...]
    f = w[:, 0:1] * rows_ref[0]
    for k in range(1, TOP_K):
        f = f + w[:, k:k + 1] * rows_ref[k]
    r = lax.rsqrt(jnp.mean(f * f, axis=-1, keepdims=True) + RMS_EPS)
    y_ref[...] = x1_ref[...] + gt_ref[...] * (f * r * g_ref[...])


def _final(dest, topw, x1_all, mod4, g_post, eo, *, row0, n, seq0, seq_len):
    d = x1_all.shape[1]
    bt = min(FINAL_BT, seq_len)
    assert row0 % bt == 0 and n % bt == 0 and seq_len % bt == 0
    b0 = row0 // bt
    bps = seq_len // bt
    nt = dest.shape[0]
    dest3 = dest.reshape(nt // bt, 1, bt * TOP_K)
    return pl.pallas_call(
        functools.partial(_final_kernel, bt=bt),
        grid=(n // bt,),
        in_specs=[pl.BlockSpec((1, 1, bt * TOP_K), lambda i: (b0 + i, 0, 0), memory_space=pltpu.SMEM),
                  pl.BlockSpec((bt, LANES), lambda i: (b0 + i, 0)),
                  pl.BlockSpec((bt, d), lambda i: (b0 + i, 0)),
                  pl.BlockSpec((None, None, 1, d), lambda i: (seq0 + i // bps, 5, 0, 0)),
                  pl.BlockSpec((1, d), lambda i: (0, 0)),
                  pl.BlockSpec(memory_space=pl.ANY)],
        out_specs=pl.BlockSpec((bt, d), lambda i: (i, 0)),
        out_shape=jax.ShapeDtypeStruct((n, d), F32),
        scratch_shapes=[pltpu.VMEM((TOP_K, bt, d), F32), pltpu.SemaphoreType.DMA(())],
        compiler_params=_cparams(("arbitrary",)),
        name="final",
    )(dest3, topw, x1_all, mod4, g_post, eo)


def kernel(x_prompt, x_sample, c_prompt, c_sample, w_ada, b_ada, g_pre_mix, g_post_mix, g_pre_ffn,
           g_post_ffn, w_in, w_out, q_norm, k_norm, lam_params, subln, w_router, b_router,
           w_gate_up, b_gate_up, w_down, b_down):
    bp, sp, d = x_prompt.shape
    bs, ss, _ = x_sample.shape
    assert bp == 1, "prompt group is a single sequence"
    n_p, n_s = bp * sp, bs * ss
    nt = n_p + n_s
    l = 0
    scale = HEAD_DIM ** -0.5

    w_in_b = w_in[l].astype(BF16)
    w_out_b = w_out[l].astype(BF16)
    w_r_b = jnp.zeros((d, LANES), BF16).at[:, :N_EXPERTS].set(w_router[l].astype(BF16))
    b_r = jnp.full((1, LANES), NEG_BIG, F32).at[0, :N_EXPERTS].set(b_router[l].astype(F32))
    row = lambda g: g[l].astype(F32).reshape(1, -1)
    lp = lam_params[l].astype(F32)
    lam = (jnp.exp(jnp.sum(lp[0] * lp[1])) - jnp.exp(jnp.sum(lp[2] * lp[3])) + LAM_INIT).reshape(1)
    slopes = 2.0 ** (-8.0 * jnp.arange(1, B_HEADS + 1, dtype=F32) / B_HEADS)

    n_seq = bp + bs
    c_pad = jnp.zeros((16, d), F32).at[:n_seq].set(jnp.concatenate([c_prompt, c_sample], axis=0).astype(F32))
    mod = _ada(c_pad, w_ada[l], b_ada[l].astype(F32))
    mod4 = mod.reshape(16, N_MOD, 1, d)

    xp2 = x_prompt.reshape(n_p, d)
    xs2 = x_sample.reshape(n_s, d)
    groups = ((xp2, bp, sp, 0), (xs2, bs, ss, bp))
    mixes = []
    for x2, batch, seq_len, seq0 in groups:
        tables = (*_rope_tables(seq_len, q_norm[l], scale), *_rope_tables(seq_len, k_norm[l], 1.0))
        proj, c_a, c_b = _inproj(x2, mod4, row(g_pre_mix), w_in_b, tables, seq0=seq0, seq_len=seq_len)
        oa, ob = _attention(proj, c_a, c_b, slopes, lam, row(subln), batch, seq_len)
        mixes.append(_outmm(oa, ob, w_out_b))

    x1_all, h2p, logits = _post(mixes[0], mixes[1], xp2, xs2, mod4, row(g_post_mix), row(g_pre_ffn),
                                w_r_b, b_r, seq_s=ss, nseq_p=bp)

    idx, topw, rank, cnt = _route(logits)
    counts = cnt[0, :N_EXPERTS].astype(I32)
    padded = ((counts + MOE_BM - 1) // MOE_BM) * MOE_BM
    ends = jnp.cumsum(padded)
    pad_start = ends - padded
    dest = (pad_start[idx[:, :TOP_K]] + rank[:, :TOP_K]).astype(I32)
    n_blocks = (nt * TOP_K) // MOE_BM + N_EXPERTS
    total = n_blocks * MOE_BM
    block_starts = jnp.arange(n_blocks, dtype=I32) * MOE_BM
    block_e = jnp.minimum(jnp.sum(block_starts[:, None] >= ends[None, :], axis=1), N_EXPERTS - 1).astype(I32)
    pad_info = jnp.stack([pad_start + counts, ends]).astype(I32)

    xs = _dispatch(pad_info, dest, h2p, total)
    block_valid = jnp.clip((pad_start + counts)[block_e] - block_starts, 0, MOE_BM).astype(I32)
    prev_e = jnp.concatenate([jnp.full((1,), -1, I32), block_e[:-1]])
    first = jnp.logical_and(block_valid > 0, block_e != prev_e)
    segidx = (jnp.cumsum(first.astype(I32)) - 1).astype(I32)
    nseg = jnp.sum(first.astype(I32)).reshape(1)
    seg_e = jnp.zeros((N_EXPERTS + 1,), I32).at[jnp.where(first, segidx, N_EXPERTS)].set(block_e)
    nxt_e = seg_e[(segidx + 1) % jnp.maximum(nseg[0], 1)]
    meta = (block_e, block_valid, first.astype(I32), jnp.maximum(segidx, 0), nxt_e.astype(I32),
            nseg.astype(I32))
    assert w_gate_up.shape[0] == 1 and w_down.shape[0] == 1, "single-layer stack"
    act = _gu(meta, xs, w_gate_up.reshape(w_gate_up.shape[1:]), b_gate_up[l].astype(F32))
    eo = _down(meta, act, w_down.reshape(w_down.shape[1:]), b_down[l].astype(F32))

    y_p = _final(dest, topw, x1_all, mod4, row(g_post_ffn), eo, row0=0, n=n_p, seq0=0, seq_len=sp)
    y_s = _final(dest, topw, x1_all, mod4, row(g_post_ffn), eo, row0=n_p, n=n_s, seq0=bp, seq_len=ss)
    return y_p.reshape(bp, sp, d), y_s.reshape(bs, ss, d)
```

```python
import functools
import math

import jax
import jax.numpy as jnp
import numpy as np
from jax import lax
from jax.experimental import pallas as pl
from jax.experimental.pallas import tpu as pltpu

F32 = jnp.float32
BF16 = jnp.bfloat16
I32 = jnp.int32
U32 = jnp.uint32

HEAD_DIM = 128
GRID_W = 64
ROPE_THETA = 10000.0
RMS_EPS = 1e-6
A_HEADS = 16
A_KV_HEADS = 4
B_HEADS = 8
N_EXPERTS = 32
TOP_K = 4
SWIGLU_LIMIT = 7.0
SWIGLU_ALPHA = 1.702
N_MOD = 6
LAM_INIT = 0.8 - 0.6 * math.exp(-0.3 * 0)

LANES = 128
V7X_VMEM_BYTES = 64 * 1024 * 1024
VMEM_LIMIT = 56 * 1024 * 1024
NEG_BIG = -1e30

INPROJ_BM = 512
ATTN_A_BQ = 256
ATTN_A_BK = 512
ATTN_B_BQ = 512
ATTN_B_BK = 512
FAST_A_BQ = 256
FAST_A_BK = 2048
FAST_B_BQ = 512
FAST_B_BK = 512

SAFE_LOGIT = 40.0
EXP_ZERO_BELOW = -104.0
NORM_SLACK = 1.01
OUTMM_BM = 1024
OUTMM_BN = 512
POST_BM = 256
ROUTE_BT = 512
MOE_BM = 512
MOE_ROW_STEPS = 4
GU_TF = 512
DOWN_BN = 1024
DISPATCH_BT = 512
FINAL_BT = 256
ADA_BN = 512


def _cparams(sem):
    return pltpu.CompilerParams(dimension_semantics=("arbitrary",) * len(sem), vmem_limit_bytes=VMEM_LIMIT)


def _ada_kernel(c_ref, w_ref, b_ref, o_ref):
    c = c_ref[...]
    a = (c * jax.nn.sigmoid(c)).astype(BF16)
    o_ref[...] = jnp.dot(a, w_ref[...].astype(BF16), preferred_element_type=F32) + b_ref[...]


def _ada(c_pad, w_ada, b_ada):
    rows, d = c_pad.shape
    n = w_ada.shape[1]
    bn = min(ADA_BN, n)
    return pl.pallas_call(
        _ada_kernel,
        grid=(n // bn,),
        in_specs=[pl.BlockSpec((rows, d), lambda j: (0, 0)),
                  pl.BlockSpec((d, bn), lambda j: (0, j)),
                  pl.BlockSpec((1, bn), lambda j: (0, j))],
        out_specs=pl.BlockSpec((rows, bn), lambda j: (0, j)),
        out_shape=jax.ShapeDtypeStruct((rows, n), F32),
        compiler_params=_cparams(("parallel",)),
        name="ada",
    )(c_pad, w_ada, b_ada.reshape(1, n))


def _swap_pairs(x):
    n = x.shape[-1]
    lane = lax.broadcasted_iota(I32, x.shape, x.ndim - 1)
    up = pltpu.roll(x, n - 32, x.ndim - 1)
    dn = pltpu.roll(x, 32, x.ndim - 1)
    return jnp.where((lane % 64) < 32, up, dn)


def _inproj_kernel(x_ref, sc_ref, sh_ref, g_ref, w_ref, tqc_ref, tqs_ref, tkc_ref, tks_ref,
                   o_ref, n_ref, h_ref, *, nq, nk, qb_lo, qb_hi, kb_hi, heads_per_tile, qscale):
    j = pl.program_id(1)

    def max_sq_norm(y, best):
        n2 = jnp.max(jnp.sum(y * y, axis=-1, keepdims=True), axis=0, keepdims=True)
        return n2 if best is None else jnp.maximum(best, n2)

    @pl.when(j == 0)
    def _():
        x = x_ref[...]
        r = lax.rsqrt(jnp.mean(x * x, axis=-1, keepdims=True) + RMS_EPS)
        h = (x * r * g_ref[...]) * (1.0 + sc_ref[...]) + sh_ref[...]
        h_ref[...] = h.astype(BF16)

    acc = jnp.dot(h_ref[...], w_ref[...], preferred_element_type=F32)

    def put_norm(best):
        n_ref[...] = jnp.broadcast_to(best, n_ref.shape)

    def normed_rope(tc_ref, ts_ref):
        tc = tc_ref[...]
        ts = ts_ref[...]
        best = None
        for hh in range(heads_per_tile):
            xh = acc[:, hh * HEAD_DIM:(hh + 1) * HEAD_DIM]
            r = lax.rsqrt(jnp.mean(xh * xh, axis=-1, keepdims=True) + RMS_EPS)
            y = r * (xh * tc + _swap_pairs(xh) * ts)
            best = max_sq_norm(y, best)
            o_ref[:, hh * HEAD_DIM:(hh + 1) * HEAD_DIM] = y.astype(o_ref.dtype)
        put_norm(best)

    def scaled(factor):
        y = acc * factor
        best = None
        for hh in range(heads_per_tile):
            best = max_sq_norm(y[:, hh * HEAD_DIM:(hh + 1) * HEAD_DIM], best)
        o_ref[...] = y.astype(o_ref.dtype)
        put_norm(best)

    @pl.when(j < nq)
    def _():
        normed_rope(tqc_ref, tqs_ref)

    @pl.when(jnp.logical_and(j >= nq, j < nq + nk))
    def _():
        normed_rope(tkc_ref, tks_ref)

    @pl.when(jnp.logical_and(j >= qb_lo, j < qb_hi))
    def _():
        scaled(qscale)

    @pl.when(jnp.logical_and(j >= qb_hi, j < kb_hi))
    def _():
        scaled(1.0)

    @pl.when(jnp.logical_or(jnp.logical_and(j >= nq + nk, j < qb_lo), j >= kb_hi))
    def _():
        o_ref[...] = acc.astype(o_ref.dtype)
        n_ref[...] = jnp.zeros(n_ref.shape, F32)


def _inproj(x2d, mod4, g_pre, w_in, tables, *, seq0, seq_len):
    n, d = x2d.shape
    cols = w_in.shape[1]
    a_width = A_HEADS * HEAD_DIM
    kv_width = A_KV_HEADS * HEAD_DIM
    b_width = B_HEADS * 2 * HEAD_DIM
    bm = min(INPROJ_BM, seq_len)
    bn = min(512, kv_width)
    assert seq_len % bm == 0 and a_width % bn == 0 and kv_width % bn == 0 and b_width % bn == 0
    nq, nk = a_width // bn, kv_width // bn
    qb_lo = (a_width + 2 * kv_width) // bn
    qb_hi = qb_lo + b_width // bn
    kb_hi = qb_hi + b_width // bn
    blocks_per_seq = seq_len // bm
    seq_of = lambda i: seq0 + i // blocks_per_seq
    tab_spec = pl.BlockSpec((bm, HEAD_DIM), lambda i, j: (i % blocks_per_seq, 0))
    kernel = functools.partial(_inproj_kernel, nq=nq, nk=nk, qb_lo=qb_lo, qb_hi=qb_hi, kb_hi=kb_hi,
                               heads_per_tile=bn // HEAD_DIM, qscale=HEAD_DIM ** -0.5)
    proj, nrm = pl.pallas_call(
        kernel,
        grid=(n // bm, cols // bn),
        in_specs=[pl.BlockSpec((bm, d), lambda i, j: (i, 0)),
                  pl.BlockSpec((None, None, 1, d), lambda i, j: (seq_of(i), 1, 0, 0)),
                  pl.BlockSpec((None, None, 1, d), lambda i, j: (seq_of(i), 0, 0, 0)),
                  pl.BlockSpec((1, d), lambda i, j: (0, 0)),
                  pl.BlockSpec((d, bn), lambda i, j: (0, j)),
                  tab_spec, tab_spec, tab_spec, tab_spec],
        out_specs=[pl.BlockSpec((bm, bn), lambda i, j: (i, j)),
                   pl.BlockSpec((None, None, 8, LANES), lambda i, j: (i, j, 0, 0))],
        out_shape=[jax.ShapeDtypeStruct((n, cols), BF16),
                   jax.ShapeDtypeStruct((n // bm, cols // bn, 8, LANES), F32)],
        scratch_shapes=[pltpu.VMEM((bm, d), BF16)],
        compiler_params=_cparams(("parallel", "arbitrary")),
        name="inproj",
    )(x2d, mod4, mod4, g_pre, w_in, *tables)
    t = jnp.max(nrm[:, :, 0, 0], axis=0)
    bound = lambda qs, ks: jnp.sqrt(jnp.max(t[qs]) * jnp.max(t[ks])) * NORM_SLACK
    c_a = bound(slice(0, nq), slice(nq, nq + nk))
    c_b = bound(slice(qb_lo, qb_hi), slice(qb_hi, kb_hi))
    return proj, c_a, c_b


def _rope_tables(seq_len, gain, scale):
    half = HEAD_DIM // 2
    n_rows = seq_len // GRID_W
    inv = np.float32(ROPE_THETA) ** (-np.arange(0, half, 2, dtype=np.float32) / np.float32(half))
    ar = (np.arange(n_rows, dtype=np.float32)[:, None] * inv[None, :]).astype(np.float64)
    ac = (np.arange(GRID_W, dtype=np.float32)[:, None] * inv[None, :]).astype(np.float64)
    per_row = lambda a: jnp.broadcast_to(jnp.asarray(a, F32)[:, None, :], (n_rows, GRID_W, half // 2))
    per_col = lambda a: jnp.broadcast_to(jnp.asarray(a, F32)[None, :, :], (n_rows, GRID_W, half // 2))
    cr, sr, cc, sc = per_row(np.cos(ar)), per_row(np.sin(ar)), per_col(np.cos(ac)), per_col(np.sin(ac))
    cos = jnp.concatenate([cr, cr, cc, cc], axis=-1).reshape(seq_len, HEAD_DIM)
    sin = jnp.concatenate([-sr, sr, -sc, sc], axis=-1).reshape(seq_len, HEAD_DIM)
    g = gain.astype(F32).reshape(HEAD_DIM)
    lane = jnp.arange(HEAD_DIM)
    partner = jnp.where((lane % 64) < 32, lane + 32, lane - 32)
    return cos * (g * scale)[None, :], sin * (g[partner] * scale)[None, :]


def _online_softmax_step(s, v, m_ref, l_ref, acc_ref):
    m_prev = m_ref[...]
    m_new = jnp.maximum(m_prev, jnp.max(s, axis=-1, keepdims=True))
    alpha = jnp.exp(m_prev - m_new)
    p = jnp.exp(s - m_new)
    l_ref[...] = alpha * l_ref[...] + jnp.sum(p, axis=-1, keepdims=True)
    acc_ref[...] = alpha * acc_ref[...] + jnp.dot(p.astype(BF16), v, preferred_element_type=F32)
    m_ref[...] = m_new


_NT = (((1,), (1,)), ((), ()))


def _attn_a_kernel(q_ref, k_ref, v_ref, o_ref, q4_ref, m_ref, l_ref, acc_ref, *, group, bq, bk, nkv):
    for h in range(group):
        q4_ref[h * bq:(h + 1) * bq, :] = q_ref[:, h * HEAD_DIM:(h + 1) * HEAD_DIM]
    m_ref[...] = jnp.full(m_ref.shape, -jnp.inf, F32)
    l_ref[...] = jnp.zeros(l_ref.shape, F32)
    acc_ref[...] = jnp.zeros(acc_ref.shape, F32)

    def body(j, carry):
        off = pl.multiple_of(j * bk, bk)
        k = k_ref[pl.ds(off, bk), :]
        v = v_ref[pl.ds(off, bk), :]
        s = lax.dot_general(q4_ref[...], k, _NT, preferred_element_type=F32)
        _online_softmax_step(s, v, m_ref, l_ref, acc_ref)
        return carry

    lax.fori_loop(0, nkv, body, 0)
    o = acc_ref[...] / l_ref[...]
    for h in range(group):
        o_ref[:, h * HEAD_DIM:(h + 1) * HEAD_DIM] = o[h * bq:(h + 1) * bq, :].astype(o_ref.dtype)


def _attn_a(proj, batch, seq_len):
    group = A_HEADS // A_KV_HEADS
    a_width = A_HEADS * HEAD_DIM
    kv_width = A_KV_HEADS * HEAD_DIM
    bq = min(ATTN_A_BQ, seq_len)
    bk = min(ATTN_A_BK, seq_len)
    qblocks = seq_len // bq
    gw = group * HEAD_DIM
    k_col0 = a_width // HEAD_DIM
    v_col0 = (a_width + kv_width) // HEAD_DIM
    rows = group * bq
    kernel = functools.partial(_attn_a_kernel, group=group, bq=bq, bk=bk, nkv=seq_len // bk)
    return pl.pallas_call(
        kernel,
        grid=(batch, A_KV_HEADS, qblocks),
        in_specs=[pl.BlockSpec((bq, gw), lambda b, g, i: (b * qblocks + i, g)),
                  pl.BlockSpec((seq_len, HEAD_DIM), lambda b, g, i: (b, k_col0 + g),
                               pipeline_mode=pl.Buffered(1)),
                  pl.BlockSpec((seq_len, HEAD_DIM), lambda b, g, i: (b, v_col0 + g),
                               pipeline_mode=pl.Buffered(1))],
        out_specs=pl.BlockSpec((bq, gw), lambda b, g, i: (b * qblocks + i, g)),
        out_shape=jax.ShapeDtypeStruct((batch * seq_len, a_width), BF16),
        scratch_shapes=[pltpu.VMEM((rows, HEAD_DIM), BF16),
                        pltpu.VMEM((rows, 1), F32),
                        pltpu.VMEM((rows, 1), F32),
                        pltpu.VMEM((rows, HEAD_DIM), F32)],
        compiler_params=_cparams(("parallel", "parallel", "parallel")),
        name="attn_a",
    )(proj, proj, proj)


def _attn_b_kernel(slope_ref, lam_ref, q_ref, k_ref, v_ref, subln_ref, o_ref, m_ref, l_ref, acc_ref,
                   *, bq, bk, nkv):
    h = pl.program_id(1)
    i = pl.program_id(2)
    slope = slope_ref[h]
    lam = lam_ref[0]
    q = q_ref[...]
    q1 = q[:, :HEAD_DIM]
    q2 = q[:, HEAD_DIM:]
    qpos = (i * bq + lax.broadcasted_iota(I32, (bq, 1), 0)).astype(F32)
    m_ref[...] = jnp.full(m_ref.shape, -jnp.inf, F32)
    l_ref[...] = jnp.zeros(l_ref.shape, F32)
    acc_ref[...] = jnp.zeros(acc_ref.shape, F32)

    def body(j, carry):
        off = pl.multiple_of(j * bk, bk)
        k = k_ref[pl.ds(off, bk), :]
        v = v_ref[pl.ds(off, bk), :]
        kpos = (j * bk + lax.broadcasted_iota(I32, (1, bk), 1)).astype(F32)
        bias = -slope * jnp.abs(qpos - kpos)
        s1 = lax.dot_general(q1, k[:, :HEAD_DIM], _NT, preferred_element_type=F32) + bias
        s2 = lax.dot_general(q2, k[:, HEAD_DIM:], _NT, preferred_element_type=F32) + bias
        _online_softmax_step(jnp.concatenate([s1, s2], axis=0), v, m_ref, l_ref, acc_ref)
        return carry

    lax.fori_loop(0, nkv, body, 0)
    o = acc_ref[...] / l_ref[...]
    o = o[:bq, :] - lam * o[bq:, :]
    r = lax.rsqrt(jnp.mean(o * o, axis=-1, keepdims=True) + RMS_EPS)
    o_ref[...] = ((o * r * subln_ref[...]) * (1.0 - LAM_INIT)).astype(o_ref.dtype)


def _attn_b(proj, slopes, lam, subln, batch, seq_len):
    a_width = A_HEADS * HEAD_DIM
    kv_width = A_KV_HEADS * HEAD_DIM
    b_width = B_HEADS * 2 * HEAD_DIM
    hw = 2 * HEAD_DIM
    bq = min(ATTN_B_BQ, seq_len)
    bk = min(ATTN_B_BK, seq_len)
    qblocks = seq_len // bq
    base = a_width + 2 * kv_width
    assert base % hw == 0
    q_col0, k_col0, v_col0 = base // hw, (base + b_width) // hw, (base + 2 * b_width) // hw
    kernel = functools.partial(_attn_b_kernel, bq=bq, bk=bk, nkv=seq_len // bk)
    grid_spec = pltpu.PrefetchScalarGridSpec(
        num_scalar_prefetch=2,
        grid=(batch, B_HEADS, qblocks),
        in_specs=[pl.BlockSpec((bq, hw), lambda b, h, i, *_: (b * qblocks + i, q_col0 + h)),
                  pl.BlockSpec((seq_len, hw), lambda b, h, i, *_: (b, k_col0 + h),
                               pipeline_mode=pl.Buffered(1)),
                  pl.BlockSpec((seq_len, hw), lambda b, h, i, *_: (b, v_col0 + h),
                               pipeline_mode=pl.Buffered(1)),
                  pl.BlockSpec((1, hw), lambda b, h, i, *_: (0, 0))],
        out_specs=pl.BlockSpec((bq, hw), lambda b, h, i, *_: (b * qblocks + i, h)),
        scratch_shapes=[pltpu.VMEM((2 * bq, 1), F32),
                        pltpu.VMEM((2 * bq, 1), F32),
                        pltpu.VMEM((2 * bq, hw), F32)],
    )
    return pl.pallas_call(
        kernel,
        grid_spec=grid_spec,
        out_shape=jax.ShapeDtypeStruct((batch * seq_len, b_width), BF16),
        compiler_params=_cparams(("parallel", "parallel", "parallel")),
        name="attn_b",
    )(slopes, lam, proj, proj, proj, subln)


def _lane_partial_sum(p):
    ps = p[:, 0:LANES]
    for t in range(1, p.shape[1] // LANES):
        ps = ps + p[:, t * LANES:(t + 1) * LANES]
    return ps


def _attn_a_fast_kernel(q_ref, k_ref, v_ref, o_ref, q4_ref, l_ref, acc_ref, *, group, bq, bk, nkv):
    for h in range(group):
        q4_ref[h * bq:(h + 1) * bq, :] = q_ref[:, h * HEAD_DIM:(h + 1) * HEAD_DIM]
    l_ref[...] = jnp.zeros(l_ref.shape, F32)
    acc_ref[...] = jnp.zeros(acc_ref.shape, F32)

    def body(j, carry):
        off = pl.multiple_of(j * bk, bk)
        k = k_ref[pl.ds(off, bk), :]
        v = v_ref[pl.ds(off, bk), :]
        p = jnp.exp(lax.dot_general(q4_ref[...], k, _NT, preferred_element_type=F32))
        l_ref[...] += _lane_partial_sum(p)
        acc_ref[...] += jnp.dot(p.astype(BF16), v, preferred_element_type=F32)
        return carry

    lax.fori_loop(0, nkv, body, 0)
    o = acc_ref[...] / jnp.sum(l_ref[...], axis=-1, keepdims=True)
    for h in range(group):
        o_ref[:, h * HEAD_DIM:(h + 1) * HEAD_DIM] = o[h * bq:(h + 1) * bq, :].astype(o_ref.dtype)


def _attn_a_fast(proj, batch, seq_len):
    group = A_HEADS // A_KV_HEADS
    a_width = A_HEADS * HEAD_DIM
    kv_width = A_KV_HEADS * HEAD_DIM
    bq = min(FAST_A_BQ, seq_len)
    bk = min(FAST_A_BK, seq_len)
    qblocks = seq_len // bq
    gw = group * HEAD_DIM
    k_col0 = a_width // HEAD_DIM
    v_col0 = (a_width + kv_width) // HEAD_DIM
    rows = group * bq
    kernel = functools.partial(_attn_a_fast_kernel, group=group, bq=bq, bk=bk, nkv=seq_len // bk)
    return pl.pallas_call(
        kernel,
        grid=(batch, A_KV_HEADS, qblocks),
        in_specs=[pl.BlockSpec((bq, gw), lambda b, g, i: (b * qblocks + i, g)),
                  pl.BlockSpec((seq_len, HEAD_DIM), lambda b, g, i: (b, k_col0 + g),
                               pipeline_mode=pl.Buffered(1)),
                  pl.BlockSpec((seq_len, HEAD_DIM), lambda b, g, i: (b, v_col0 + g),
                               pipeline_mode=pl.Buffered(1))],
        out_specs=pl.BlockSpec((bq, gw), lambda b, g, i: (b * qblocks + i, g)),
        out_shape=jax.ShapeDtypeStruct((batch * seq_len, a_width), BF16),
        scratch_shapes=[pltpu.VMEM((rows, HEAD_DIM), BF16),
                        pltpu.VMEM((rows, LANES), F32),
                        pltpu.VMEM((rows, HEAD_DIM), F32)],
        compiler_params=_cparams(("parallel", "parallel", "parallel")),
        name="attn_a_fast",
    )(proj, proj, proj)


def _attn_b_fast_kernel(slope_ref, lam_ref, reach_ref, q_ref, k_ref, v_ref, subln_ref, o_ref, l_ref, acc_ref,
                        p_ref, *, bq, bk, nkv):
    h = pl.program_id(1)
    i = pl.program_id(2)
    slope = slope_ref[h]
    lam = lam_ref[0]
    reach = reach_ref[h]
    q = q_ref[...]
    q1 = q[:, :HEAD_DIM]
    q2 = q[:, HEAD_DIM:]
    i0 = i * bq
    qpos = (i0 + lax.broadcasted_iota(I32, (bq, 1), 0)).astype(F32)
    l_ref[...] = jnp.zeros(l_ref.shape, F32)
    acc_ref[...] = jnp.zeros(acc_ref.shape, F32)
    jb_lo = jnp.maximum(i0 - reach, 0) // bk
    jb_hi = jnp.minimum((i0 + bq - 1 + reach) // bk + 1, nkv)

    odd = (jb_hi - jb_lo) % 2
    grow_hi = jnp.logical_and(odd == 1, jb_hi < nkv).astype(I32)
    jb_hi = jb_hi + grow_hi
    jb_lo = jb_lo - (odd - grow_hi)
    npair = (jb_hi - jb_lo) // 2

    def probs(j, slot):
        off = pl.multiple_of(j * bk, bk)
        k = k_ref[pl.ds(off, bk), :]
        kpos = (j * bk + lax.broadcasted_iota(I32, (1, bk), 1)).astype(F32)
        bias = -slope * jnp.abs(qpos - kpos)
        p1 = jnp.exp(lax.dot_general(q1, k[:, :HEAD_DIM], _NT, preferred_element_type=F32) + bias)
        p2 = jnp.exp(lax.dot_general(q2, k[:, HEAD_DIM:], _NT, preferred_element_type=F32) + bias)
        l_ref[:bq, :] += _lane_partial_sum(p1)
        l_ref[bq:, :] += _lane_partial_sum(p2)
        p_ref[slot, :bq, :] = p1.astype(BF16)
        p_ref[slot, bq:, :] = p2.astype(BF16)

    def weighted_values(j, slot):
        off = pl.multiple_of(j * bk, bk)
        acc_ref[...] += jnp.dot(p_ref[slot], v_ref[pl.ds(off, bk), :], preferred_element_type=F32)

    probs(jb_lo, 0)

    def body(t, carry):
        j = jb_lo + 2 * t
        weighted_values(j, 0)
        probs(j + 1, 1)
        weighted_values(j + 1, 1)
        probs(j + 2, 0)
        return carry

    lax.fori_loop(0, npair - 1, body, 0)
    weighted_values(jb_hi - 2, 0)
    probs(jb_hi - 1, 1)
    weighted_values(jb_hi - 1, 1)

    o = acc_ref[...] / jnp.sum(l_ref[...], axis=-1, keepdims=True)
    o = o[:bq, :] - lam * o[bq:, :]
    r = lax.rsqrt(jnp.mean(o * o, axis=-1, keepdims=True) + RMS_EPS)
    o_ref[...] = ((o * r * subln_ref[...]) * (1.0 - LAM_INIT)).astype(o_ref.dtype)


def _attn_b_fast(proj, slopes, lam, reach, subln, batch, seq_len):
    a_width = A_HEADS * HEAD_DIM
    kv_width = A_KV_HEADS * HEAD_DIM
    b_width = B_HEADS * 2 * HEAD_DIM
    hw = 2 * HEAD_DIM
    bq = min(FAST_B_BQ, seq_len)
    bk = min(FAST_B_BK, seq_len // 2)
    qblocks = seq_len // bq
    base = a_width + 2 * kv_width
    assert base % hw == 0 and (seq_len // bk) % 2 == 0
    q_col0, k_col0, v_col0 = base // hw, (base + b_width) // hw, (base + 2 * b_width) // hw
    kernel = functools.partial(_attn_b_fast_kernel, bq=bq, bk=bk, nkv=seq_len // bk)
    grid_spec = pltpu.PrefetchScalarGridSpec(
        num_scalar_prefetch=3,
        grid=(batch, B_HEADS, qblocks),
        in_specs=[pl.BlockSpec((bq, hw), lambda b, h, i, *_: (b * qblocks + i, q_col0 + h)),
                  pl.BlockSpec((seq_len, hw), lambda b, h, i, *_: (b, k_col0 + h),
                               pipeline_mode=pl.Buffered(1)),
                  pl.BlockSpec((seq_len, hw), lambda b, h, i, *_: (b, v_col0 + h),
                               pipeline_mode=pl.Buffered(1)),
                  pl.BlockSpec((1, hw), lambda b, h, i, *_: (0, 0))],
        out_specs=pl.BlockSpec((bq, hw), lambda b, h, i, *_: (b * qblocks + i, h)),
        scratch_shapes=[pltpu.VMEM((2 * bq, LANES), F32),
                        pltpu.VMEM((2 * bq, hw), F32),
                        pltpu.VMEM((2, 2 * bq, bk), BF16)],
    )
    return pl.pallas_call(
        kernel,
        grid_spec=grid_spec,
        out_shape=jax.ShapeDtypeStruct((batch * seq_len, b_width), BF16),
        compiler_params=_cparams(("parallel", "parallel", "parallel")),
        name="attn_b_fast",
    )(slopes, lam, reach, proj, proj, proj, subln)


def _attention(proj, c_a, c_b, slopes, lam, subln, batch, seq_len):
    oa = lax.cond(c_a <= SAFE_LOGIT,
                  lambda p: _attn_a_fast(p, batch, seq_len),
                  lambda p: _attn_a(p, batch, seq_len), proj)
    reach = jnp.clip(jnp.ceil((c_b - EXP_ZERO_BELOW) / slopes), 0, seq_len).astype(I32)
    ob = lax.cond(c_b <= SAFE_LOGIT,
                  lambda p: _attn_b_fast(p, slopes, lam, reach, subln, batch, seq_len),
                  lambda p: _attn_b(p, slopes, lam, subln, batch, seq_len), proj)
    return oa, ob


def _outmm_kernel(a1_ref, a2_ref, w1_ref, w2_ref, o_ref):
    o_ref[...] = (jnp.dot(a1_ref[...], w1_ref[...], preferred_element_type=F32)
                  + jnp.dot(a2_ref[...], w2_ref[...], preferred_element_type=F32))


def _outmm(oa, ob, w_out):
    n, ka = oa.shape
    kb = ob.shape[1]
    d = w_out.shape[1]
    bm = min(OUTMM_BM, n)
    bn = min(OUTMM_BN, d)
    assert ka == kb and ka % 16 == 0
    return pl.pallas_call(
        _outmm_kernel,
        grid=(n // bm, d // bn),
        in_specs=[pl.BlockSpec((bm, ka), lambda i, j: (i, 0)),
                  pl.BlockSpec((bm, kb), lambda i, j: (i, 0)),
                  pl.BlockSpec((ka, bn), lambda i, j: (0, j)),
                  pl.BlockSpec((kb, bn), lambda i, j: (1, j))],
        out_specs=pl.BlockSpec((bm, bn), lambda i, j: (i, j)),
        out_shape=jax.ShapeDtypeStruct((n, d), F32),
        compiler_params=_cparams(("parallel", "parallel")),
        name="outmm",
    )(oa, ob, w_out, w_out)


def _post_body(mix_ref, x_ref, gt_ref, sc_ref, sh_ref, gpost_ref, gpre_ref, wr_ref, br_ref,
               x1_ref, h2_ref, lg_ref):
    mix = mix_ref[...]
    r = lax.rsqrt(jnp.mean(mix * mix, axis=-1, keepdims=True) + RMS_EPS)
    x1 = x_ref[...] + gt_ref[...] * (mix * r * gpost_ref[...])
    x1_ref[...] = x1
    r2 = lax.rsqrt(jnp.mean(x1 * x1, axis=-1, keepdims=True) + RMS_EPS)
    h = (x1 * r2 * gpre_ref[...]) * (1.0 + sc_ref[...]) + sh_ref[...]
    hb = h.astype(BF16)
    half = h.shape[1] // 2
    hr = hb.astype(F32)
    lo = lax.bitcast_convert_type(hr[:, :half], U32) >> 16
    hi = lax.bitcast_convert_type(hr[:, half:], U32) & jnp.uint32(0xFFFF0000)
    h2_ref[...] = hi | lo
    lg_ref[...] = jnp.dot(hb, wr_ref[...], preferred_element_type=F32) + br_ref[...]


def _post_kernel(mixp_ref, mixs_ref, xp_ref, xs_ref, gt_ref, sc_ref, sh_ref, gpost_ref, gpre_ref,
                 wr_ref, br_ref, x1_ref, h2_ref, lg_ref, *, nbp):
    i = pl.program_id(0)
    rest = (gt_ref, sc_ref, sh_ref, gpost_ref, gpre_ref, wr_ref, br_ref, x1_ref, h2_ref, lg_ref)

    @pl.when(i < nbp)
    def _():
        _post_body(mixp_ref, xp_ref, *rest)

    @pl.when(i >= nbp)
    def _():
        _post_body(mixs_ref, xs_ref, *rest)


def _post(mix_p, mix_s, x_p, x_s, mod4, g_post, g_pre, w_router, b_router, *, seq_s, nseq_p):
    n_p, d = x_p.shape
    n_s = x_s.shape[0]
    bm = min(POST_BM, seq_s)
    assert n_p % bm == 0 and n_s % bm == 0 and seq_s % bm == 0
    nbp, nbs = n_p // bm, n_s // bm
    bps = seq_s // bm
    nt = n_p + n_s
    seq_of = lambda i: jnp.where(i < nbp, 0, nseq_p + (i - nbp) // bps)
    p_idx = lambda i: (jnp.minimum(i, nbp - 1), 0)
    s_idx = lambda i: (jnp.maximum(i - nbp, 0), 0)
    mod_spec = lambda which: pl.BlockSpec((None, None, 1, d), lambda i: (seq_of(i), which, 0, 0))
    row_spec = lambda c: pl.BlockSpec((bm, c), lambda i: (i, 0))
    return pl.pallas_call(
        functools.partial(_post_kernel, nbp=nbp),
        grid=(nbp + nbs,),
        in_specs=[pl.BlockSpec((bm, d), p_idx), pl.BlockSpec((bm, d), s_idx),
                  pl.BlockSpec((bm, d), p_idx), pl.BlockSpec((bm, d), s_idx),
                  mod_spec(2), mod_spec(4), mod_spec(3),
                  pl.BlockSpec((1, d), lambda i: (0, 0)),
                  pl.BlockSpec((1, d), lambda i: (0, 0)),
                  pl.BlockSpec((d, LANES), lambda i: (0, 0)),
                  pl.BlockSpec((1, LANES), lambda i: (0, 0))],
        out_specs=[row_spec(d), row_spec(d // 2), row_spec(LANES)],
        out_shape=[jax.ShapeDtypeStruct((nt, d), F32),
                   jax.ShapeDtypeStruct((nt, d // 2), U32),
                   jax.ShapeDtypeStruct((nt, LANES), F32)],
        compiler_params=_cparams(("parallel",)),
        name="post",
    )(mix_p, mix_s, x_p, x_s, mod4, mod4, mod4, g_post, g_pre, w_router, b_router)


def _route_kernel(lg_ref, idx_ref, w_ref, rank_ref, cnt_ref, carry_ref):
    i = pl.program_id(0)

    @pl.when(i == 0)
    def _():
        carry_ref[...] = jnp.zeros(carry_ref.shape, F32)

    lg = lg_ref[...]
    bt = lg.shape[0]
    lane = lax.broadcasted_iota(I32, lg.shape, 1)
    work = lg
    vals, idxs = [], []
    onehot = jnp.zeros(lg.shape, F32)
    for _ in range(TOP_K):
        mx = jnp.max(work, axis=-1, keepdims=True)
        ix = jnp.min(jnp.where(work == mx, lane, LANES), axis=-1, keepdims=True)
        sel = lane == ix
        vals.append(mx)
        idxs.append(ix)
        work = jnp.where(sel, -jnp.inf, work)
        onehot = onehot + sel.astype(F32)
    exps = [jnp.exp(v - vals[0]) for v in vals]
    denom = exps[0]
    for e in exps[1:]:
        denom = denom + e
    tri = (lax.broadcasted_iota(I32, (bt, bt), 0) > lax.broadcasted_iota(I32, (bt, bt), 1)).astype(BF16)
    prefix = jnp.dot(tri, onehot.astype(BF16), preferred_element_type=F32) + carry_ref[...]
    idx_out = jnp.zeros(lg.shape, I32)
    w_out = jnp.zeros(lg.shape, F32)
    rank_out = jnp.zeros(lg.shape, I32)
    for k in range(TOP_K):
        rk = jnp.sum(jnp.where(lane == idxs[k], prefix, 0.0), axis=-1, keepdims=True)
        idx_out = jnp.where(lane == k, idxs[k], idx_out)
        w_out = jnp.where(lane == k, exps[k] / denom, w_out)
        rank_out = jnp.where(lane == k, rk.astype(I32), rank_out)
    idx_ref[...] = idx_out
    w_ref[...] = w_out
    rank_ref[...] = rank_out
    carry_ref[...] = carry_ref[...] + jnp.sum(onehot, axis=0, keepdims=True)
    cnt_ref[...] = carry_ref[...]


def _route(logits):
    nt = logits.shape[0]
    bt = min(ROUTE_BT, nt)
    spec = pl.BlockSpec((bt, LANES), lambda i: (i, 0))
    return pl.pallas_call(
        _route_kernel,
        grid=(nt // bt,),
        in_specs=[spec],
        out_specs=[spec, spec, spec, pl.BlockSpec((1, LANES), lambda i: (0, 0))],
        out_shape=[jax.ShapeDtypeStruct((nt, LANES), I32),
                   jax.ShapeDtypeStruct((nt, LANES), F32),
                   jax.ShapeDtypeStruct((nt, LANES), I32),
                   jax.ShapeDtypeStruct((1, LANES), F32)],
        scratch_shapes=[pltpu.VMEM((1, LANES), F32)],
        compiler_params=_cparams(("arbitrary",)),
        name="route",
    )(logits)


def _dispatch_kernel(pad_ref, dest_ref, h_ref, xs_ref, zero_ref, sem, *, bt, n_experts, total):
    i = pl.program_id(0)

    def row_copy(src, r, d):
        return pltpu.make_async_copy(src.at[pl.ds(r, 1), :], xs_ref.at[pl.ds(d, 1), :], sem)

    @pl.when(i == 0)
    def _():
        zero_ref[...] = jnp.zeros(zero_ref.shape, zero_ref.dtype)

        def fill(lo, hi):
            def start(r, c):
                row_copy(zero_ref, 0, r).start()
                return c

            def wait(r, c):
                row_copy(zero_ref, 0, r).wait()
                return c

            lax.fori_loop(lo, hi, start, 0)
            lax.fori_loop(lo, hi, wait, 0)

        def per_expert(e, c):
            fill(pad_ref[0, e], pad_ref[1, e])
            return c

        lax.fori_loop(0, n_experts, per_expert, 0)
        fill(pad_ref[1, n_experts - 1], total)

    def start(t, c):
        for k in range(TOP_K):
            row_copy(h_ref, t, dest_ref[0, 0, t * TOP_K + k]).start(priority=k % 2)
        return c

    def wait(t, c):
        for k in range(TOP_K):
            row_copy(h_ref, t, dest_ref[0, 0, t * TOP_K + k]).wait()
        return c

    lax.fori_loop(0, bt, start, 0)
    lax.fori_loop(0, bt, wait, 0)


def _dispatch(pad_info, dest, h2p, total):
    nt, half = h2p.shape
    bt = min(DISPATCH_BT, nt)
    dest3 = dest.reshape(nt // bt, 1, bt * TOP_K)
    kernel = functools.partial(_dispatch_kernel, bt=bt, n_experts=N_EXPERTS, total=total)
    grid_spec = pltpu.PrefetchScalarGridSpec(
        num_scalar_prefetch=1,
        grid=(nt // bt,),
        in_specs=[pl.BlockSpec((1, 1, bt * TOP_K), lambda i, *_: (i, 0, 0), memory_space=pltpu.SMEM),
                  pl.BlockSpec((bt, half), lambda i, *_: (i, 0))],
        out_specs=pl.BlockSpec(memory_space=pl.ANY),
        scratch_shapes=[pltpu.VMEM((8, half), U32), pltpu.SemaphoreType.DMA(())],
    )
    return pl.pallas_call(
        kernel,
        grid_spec=grid_spec,
        out_shape=jax.ShapeDtypeStruct((total, half), U32),
        compiler_params=_cparams(("arbitrary",)),
        name="dispatch",
    )(pad_info, dest3, h2p)


def _unpack_rows(p):
    lo = lax.bitcast_convert_type(p << 16, F32)
    hi = lax.bitcast_convert_type(p & jnp.uint32(0xFFFF0000), F32)
    return lo.astype(BF16), hi.astype(BF16)


def _when_rows_valid(valid, bm, run):
    quarter = bm // MOE_ROW_STEPS
    for k in range(1, MOE_ROW_STEPS + 1):
        @pl.when(jnp.logical_and(valid > (k - 1) * quarter, valid <= k * quarter))
        def _(k=k):
            run(k * quarter)


def _segment_weights(meta, copies, n_outer):
    be, bv, first, segidx, nxt, nseg = meta
    j = pl.program_id(0)
    i = pl.program_id(1)
    g = j * nseg[0] + segidx[i]
    slot = g % 2

    @pl.when(jnp.logical_and(bv[i] > 0, first[i] == 1))
    def _():
        @pl.when(g == 0)
        def _():
            for c in copies(be[i], j, slot):
                c.start()

        for c in copies(be[i], j, slot):
            c.wait()
        last = (segidx[i] == nseg[0] - 1).astype(I32)

        @pl.when(jnp.logical_not(jnp.logical_and(last == 1, j == n_outer - 1)))
        def _():
            for c in copies(nxt[i], j + last, 1 - slot):
                c.start()

    return slot


def _gu_kernel(be, bv, first, segidx, nxt, nseg, xs_ref, w_hbm, bg_ref, bu_ref, o_ref, wbuf, sem,
               *, nft, tf, f):
    def copies(e, j, slot):
        col = pl.multiple_of(j * tf, tf)
        return [pltpu.make_async_copy(w_hbm.at[e, :, pl.ds(col, tf)], wbuf.at[slot, 0], sem.at[slot]),
                pltpu.make_async_copy(w_hbm.at[e, :, pl.ds(f + col, tf)], wbuf.at[slot, 1], sem.at[slot])]

    slot = _segment_weights((be, bv, first, segidx, nxt, nseg), copies, nft)

    def run(rows):
        lo, hi = _unpack_rows(xs_ref[:rows, :])
        half = lo.shape[1]

        def proj(which, b_ref):
            w = wbuf.at[slot, which]
            return (jnp.dot(lo, w[:half, :].astype(BF16), preferred_element_type=F32)
                    + jnp.dot(hi, w[half:, :].astype(BF16), preferred_element_type=F32) + b_ref[...])

        g = jnp.minimum(proj(0, bg_ref), SWIGLU_LIMIT)
        u = jnp.clip(proj(1, bu_ref), -SWIGLU_LIMIT, SWIGLU_LIMIT)
        o_ref[:rows, :] = ((u + 1.0) * (g * jax.nn.sigmoid(SWIGLU_ALPHA * g))).astype(o_ref.dtype)

    _when_rows_valid(bv[pl.program_id(1)], xs_ref.shape[0], run)


def _gu(meta, xs, w_gu, b_gu):
    total, half = xs.shape
    d = 2 * half
    n_e, _, two_f = w_gu.shape
    f = two_f // 2
    bm = MOE_BM
    tf = min(GU_TF, f)
    nft = f // tf
    grid_spec = pltpu.PrefetchScalarGridSpec(
        num_scalar_prefetch=6,
        grid=(nft, total // bm),
        in_specs=[pl.BlockSpec((bm, half), lambda j, i, *_: (i, 0)),
                  pl.BlockSpec(memory_space=pl.ANY),
                  pl.BlockSpec((None, 1, tf), lambda j, i, be, *_: (be[i], 0, j)),
                  pl.BlockSpec((None, 1, tf), lambda j, i, be, *_: (be[i], 0, nft + j))],
        out_specs=pl.BlockSpec((bm, tf), lambda j, i, *_: (i, j)),
        scratch_shapes=[pltpu.VMEM((2, 2, d, tf), F32), pltpu.SemaphoreType.DMA((2,))],
    )
    return pl.pallas_call(
        functools.partial(_gu_kernel, nft=nft, tf=tf, f=f),
        grid_spec=grid_spec,
        out_shape=jax.ShapeDtypeStruct((total, f), BF16),
        compiler_params=_cparams(("arbitrary", "arbitrary")),
        name="moe_gu",
    )(*meta, xs, w_gu, b_gu.reshape(n_e, 1, two_f), b_gu.reshape(n_e, 1, two_f))


def _down_kernel(be, bv, first, segidx, nxt, nseg, a_ref, w_hbm, b_ref, o_ref, wbuf, sem, *, nnt, bn):
    def copies(e, j, slot):
        col = pl.multiple_of(j * bn, bn)
        return [pltpu.make_async_copy(w_hbm.at[e, :, pl.ds(col, bn)], wbuf.at[slot], sem.at[slot])]

    slot = _segment_weights((be, bv, first, segidx, nxt, nseg), copies, nnt)

    def run(rows):
        w = wbuf[slot].astype(BF16)
        o_ref[:rows, :] = jnp.dot(a_ref[:rows, :], w, preferred_element_type=F32) + b_ref[...]

    _when_rows_valid(bv[pl.program_id(1)], a_ref.shape[0], run)


def _down(meta, act, w_down, b_down):
    total, f = act.shape
    n_e, _, d = w_down.shape
    bm = MOE_BM
    bn = min(DOWN_BN, d)
    nnt = d // bn
    grid_spec = pltpu.PrefetchScalarGridSpec(
        num_scalar_prefetch=6,
        grid=(nnt, total // bm),
        in_specs=[pl.BlockSpec((bm, f), lambda j, i, *_: (i, 0)),
                  pl.BlockSpec(memory_space=pl.ANY),
                  pl.BlockSpec((None, 1, bn), lambda j, i, be, *_: (be[i], 0, j))],
        out_specs=pl.BlockSpec((bm, bn), lambda j, i, *_: (i, j)),
        scratch_shapes=[pltpu.VMEM((2, f, bn), F32), pltpu.SemaphoreType.DMA((2,))],
    )
    return pl.pallas_call(
        functools.partial(_down_kernel, nnt=nnt, bn=bn),
        grid_spec=grid_spec,
        out_shape=jax.ShapeDtypeStruct((total, d), F32),
        compiler_params=_cparams(("arbitrary", "arbitrary")),
        name="moe_down",
    )(*meta, act, w_down, b_down.reshape(n_e, 1, d))


def _final_kernel(dest_ref, w_ref, x1_ref, gt_ref, g_ref, eo_ref, y_ref, rows_ref, sem, *, bt):
    def row_copy(t, k):
        d = dest_ref[0, 0, t * TOP_K + k]
        return pltpu.make_async_copy(eo_ref.at[pl.ds(d, 1), :], rows_ref.at[k, pl.ds(t, 1), :], sem)

    def start(t, c):
        for k in range(TOP_K):
            row_copy(t, k).start()
        return c

    def wait(t, c):
        for k in range(TOP_K):
            row_copy(t, k).wait()
        return c

    lax.fori_loop(0, bt, start, 0)
    lax.fori_loop(0, bt, wait, 0)
    w = w_ref[...]
    f = w[:, 0:1] * rows_ref[0]
    for k in range(1, TOP_K):
        f = f + w[:, k:k + 1] * rows_ref[k]
    r = lax.rsqrt(jnp.mean(f * f, axis=-1, keepdims=True) + RMS_EPS)
    y_ref[...] = x1_ref[...] + gt_ref[...] * (f * r * g_ref[...])


def _final(dest, topw, x1_all, mod4, g_post, eo, *, row0, n, seq0, seq_len):
    d = x1_all.shape[1]
    bt = min(FINAL_BT, seq_len)
    assert row0 % bt == 0 and n % bt == 0 and seq_len % bt == 0
    b0 = row0 // bt
    bps = seq_len // bt
    nt = dest.shape[0]
    dest3 = dest.reshape(nt // bt, 1, bt * TOP_K)
    return pl.pallas_call(
        functools.partial(_final_kernel, bt=bt),
        grid=(n // bt,),
        in_specs=[pl.BlockSpec((1, 1, bt * TOP_K), lambda i: (b0 + i, 0, 0), memory_space=pltpu.SMEM),
                  pl.BlockSpec((bt, LANES), lambda i: (b0 + i, 0)),
                  pl.BlockSpec((bt, d), lambda i: (b0 + i, 0)),
                  pl.BlockSpec((None, None, 1, d), lambda i: (seq0 + i // bps, 5, 0, 0)),
                  pl.BlockSpec((1, d), lambda i: (0, 0)),
                  pl.BlockSpec(memory_space=pl.ANY)],
        out_specs=pl.BlockSpec((bt, d), lambda i: (i, 0)),
        out_shape=jax.ShapeDtypeStruct((n, d), F32),
        scratch_shapes=[pltpu.VMEM((TOP_K, bt, d), F32), pltpu.SemaphoreType.DMA(())],
        compiler_params=_cparams(("arbitrary",)),
        name="final",
    )(dest3, topw, x1_all, mod4, g_post, eo)


def kernel(x_prompt, x_sample, c_prompt, c_sample, w_ada, b_ada, g_pre_mix, g_post_mix, g_pre_ffn,
           g_post_ffn, w_in, w_out, q_norm, k_norm, lam_params, subln, w_router, b_router,
           w_gate_up, b_gate_up, w_down, b_down):
    bp, sp, d = x_prompt.shape
    bs, ss, _ = x_sample.shape
    assert bp == 1, "prompt group is a single sequence"
    n_p, n_s = bp * sp, bs * ss
    nt = n_p + n_s
    l = 0
    scale = HEAD_DIM ** -0.5

    w_in_b = w_in[l].astype(BF16)
    w_out_b = w_out[l].astype(BF16)
    w_r_b = jnp.zeros((d, LANES), BF16).at[:, :N_EXPERTS].set(w_router[l].astype(BF16))
    b_r = jnp.full((1, LANES), NEG_BIG, F32).at[0, :N_EXPERTS].set(b_router[l].astype(F32))
    row = lambda g: g[l].astype(F32).reshape(1, -1)
    lp = lam_params[l].astype(F32)
    lam = (jnp.exp(jnp.sum(lp[0] * lp[1])) - jnp.exp(jnp.sum(lp[2] * lp[3])) + LAM_INIT).reshape(1)
    slopes = 2.0 ** (-8.0 * jnp.arange(1, B_HEADS + 1, dtype=F32) / B_HEADS)

    n_seq = bp + bs
    c_pad = jnp.zeros((16, d), F32).at[:n_seq].set(jnp.concatenate([c_prompt, c_sample], axis=0).astype(F32))
    mod = _ada(c_pad, w_ada[l], b_ada[l].astype(F32))
    mod4 = mod.reshape(16, N_MOD, 1, d)

    xp2 = x_prompt.reshape(n_p, d)
    xs2 = x_sample.reshape(n_s, d)
    groups = ((xp2, bp, sp, 0), (xs2, bs, ss, bp))
    mixes = []
    for x2, batch, seq_len, seq0 in groups:
        tables = (*_rope_tables(seq_len, q_norm[l], scale), *_rope_tables(seq_len, k_norm[l], 1.0))
        proj, c_a, c_b = _inproj(x2, mod4, row(g_pre_mix), w_in_b, tables, seq0=seq0, seq_len=seq_len)
        oa, ob = _attention(proj, c_a, c_b, slopes, lam, row(subln), batch, seq_len)
        mixes.append(_outmm(oa, ob, w_out_b))

    x1_all, h2p, logits = _post(mixes[0], mixes[1], xp2, xs2, mod4, row(g_post_mix), row(g_pre_ffn),
                                w_r_b, b_r, seq_s=ss, nseq_p=bp)

    idx, topw, rank, cnt = _route(logits)
    counts = cnt[0, :N_EXPERTS].astype(I32)
    padded = ((counts + MOE_BM - 1) // MOE_BM) * MOE_BM
    ends = jnp.cumsum(padded)
    pad_start = ends - padded
    dest = (pad_start[idx[:, :TOP_K]] + rank[:, :TOP_K]).astype(I32)
    n_blocks = (nt * TOP_K) // MOE_BM + N_EXPERTS
    total = n_blocks * MOE_BM
    block_starts = jnp.arange(n_blocks, dtype=I32) * MOE_BM
    block_e = jnp.minimum(jnp.sum(block_starts[:, None] >= ends[None, :], axis=1), N_EXPERTS - 1).astype(I32)
    pad_info = jnp.stack([pad_start + counts, ends]).astype(I32)

    xs = _dispatch(pad_info, dest, h2p, total)
    block_valid = jnp.clip((pad_start + counts)[block_e] - block_starts, 0, MOE_BM).astype(I32)
    prev_e = jnp.concatenate([jnp.full((1,), -1, I32), block_e[:-1]])
    first = jnp.logical_and(block_valid > 0, block_e != prev_e)
    segidx = (jnp.cumsum(first.astype(I32)) - 1).astype(I32)
    nseg = jnp.sum(first.astype(I32)).reshape(1)
    seg_e = jnp.zeros((N_EXPERTS + 1,), I32).at[jnp.where(first, segidx, N_EXPERTS)].set(block_e)
    nxt_e = seg_e[(segidx + 1) % jnp.maximum(nseg[0], 1)]
    meta = (block_e, block_valid, first.astype(I32), jnp.maximum(segidx, 0), nxt_e.astype(I32),
            nseg.astype(I32))
    assert w_gate_up.shape[0] == 1 and w_down.shape[0] == 1, "single-layer stack"
    act = _gu(meta, xs, w_gate_up.reshape(w_gate_up.shape[1:]), b_gate_up[l].astype(F32))
    eo = _down(meta, act, w_down.reshape(w_down.shape[1:]), b_down[l].astype(F32))

    y_p = _final(dest, topw, x1_all, mod4, row(g_post_ffn), eo, row0=0, n=n_p, seq0=0, seq_len=sp)
    y_s = _final(dest, topw, x1_all, mod4, row(g_post_ffn), eo, row0=n_p, n=n_s, seq0=bp, seq_len=ss)
    return y_p.reshape(bp, sp, d), y_s.reshape(bs, ss, d)
```

```python
import functools
import math

import jax
import jax.numpy as jnp
import numpy as np
from jax import lax
from jax.experimental import pallas as pl
from jax.experimental.pallas import tpu as pltpu

F32 = jnp.float32
BF16 = jnp.bfloat16
I32 = jnp.int32
U32 = jnp.uint32

HEAD_DIM = 128
GRID_W = 64
ROPE_THETA = 10000.0
RMS_EPS = 1e-6
A_HEADS = 16
A_KV_HEADS = 4
B_HEADS = 8
N_EXPERTS = 32
TOP_K = 4
SWIGLU_LIMIT = 7.0
SWIGLU_ALPHA = 1.702
N_MOD = 6
LAM_INIT = 0.8 - 0.6 * math.exp(-0.3 * 0)

LANES = 128
V7X_VMEM_BYTES = 64 * 1024 * 1024
VMEM_LIMIT = 56 * 1024 * 1024
NEG_BIG = -1e30

INPROJ_BM = 512
ATTN_A_BQ = 256
ATTN_A_BK = 512
ATTN_B_BQ = 512
ATTN_B_BK = 512
FAST_A_BQ = 256
FAST_A_BK = 2048
FAST_B_BQ = 512
FAST_B_BK = 512

SAFE_LOGIT = 40.0
EXP_ZERO_BELOW = -104.0
NORM_SLACK = 1.01
OUTMM_BM = 1024
OUTMM_BN = 512
POST_BM = 256
ROUTE_BT = 512
MOE_BM = 512
GU_TF = 512
DOWN_BN = 1024
DISPATCH_BT = 256
FINAL_BT = 256
ADA_BN = 512


def _cparams(sem):
    return pltpu.CompilerParams(dimension_semantics=("arbitrary",) * len(sem), vmem_limit_bytes=VMEM_LIMIT)


def _ada_kernel(c_ref, w_ref, b_ref, o_ref):
    c = c_ref[...]
    a = (c * jax.nn.sigmoid(c)).astype(BF16)
    o_ref[...] = jnp.dot(a, w_ref[...].astype(BF16), preferred_element_type=F32) + b_ref[...]


def _ada(c_pad, w_ada, b_ada):
    rows, d = c_pad.shape
    n = w_ada.shape[1]
    bn = min(ADA_BN, n)
    return pl.pallas_call(
        _ada_kernel,
        grid=(n // bn,),
        in_specs=[pl.BlockSpec((rows, d), lambda j: (0, 0)),
                  pl.BlockSpec((d, bn), lambda j: (0, j)),
                  pl.BlockSpec((1, bn), lambda j: (0, j))],
        out_specs=pl.BlockSpec((rows, bn), lambda j: (0, j)),
        out_shape=jax.ShapeDtypeStruct((rows, n), F32),
        compiler_params=_cparams(("parallel",)),
        name="ada",
    )(c_pad, w_ada, b_ada.reshape(1, n))


def _swap_pairs(x):
    n = x.shape[-1]
    lane = lax.broadcasted_iota(I32, x.shape, x.ndim - 1)
    up = pltpu.roll(x, n - 32, x.ndim - 1)
    dn = pltpu.roll(x, 32, x.ndim - 1)
    return jnp.where((lane % 64) < 32, up, dn)


def _inproj_kernel(x_ref, sc_ref, sh_ref, g_ref, w_ref, tqc_ref, tqs_ref, tkc_ref, tks_ref,
                   o_ref, n_ref, h_ref, *, nq, nk, qb_lo, qb_hi, kb_hi, heads_per_tile, qscale):
    j = pl.program_id(1)

    def max_sq_norm(y, best):
        n2 = jnp.max(jnp.sum(y * y, axis=-1, keepdims=True), axis=0, keepdims=True)
        return n2 if best is None else jnp.maximum(best, n2)

    @pl.when(j == 0)
    def _():
        x = x_ref[...]
        r = lax.rsqrt(jnp.mean(x * x, axis=-1, keepdims=True) + RMS_EPS)
        h = (x * r * g_ref[...]) * (1.0 + sc_ref[...]) + sh_ref[...]
        h_ref[...] = h.astype(BF16)

    acc = jnp.dot(h_ref[...], w_ref[...], preferred_element_type=F32)

    def put_norm(best):
        n_ref[...] = jnp.broadcast_to(best, n_ref.shape)

    def normed_rope(tc_ref, ts_ref):
        tc = tc_ref[...]
        ts = ts_ref[...]
        best = None
        for hh in range(heads_per_tile):
            xh = acc[:, hh * HEAD_DIM:(hh + 1) * HEAD_DIM]
            r = lax.rsqrt(jnp.mean(xh * xh, axis=-1, keepdims=True) + RMS_EPS)
            y = r * (xh * tc + _swap_pairs(xh) * ts)
            best = max_sq_norm(y, best)
            o_ref[:, hh * HEAD_DIM:(hh + 1) * HEAD_DIM] = y.astype(o_ref.dtype)
        put_norm(best)

    def scaled(factor):
        y = acc * factor
        best = None
        for hh in range(heads_per_tile):
            best = max_sq_norm(y[:, hh * HEAD_DIM:(hh + 1) * HEAD_DIM], best)
        o_ref[...] = y.astype(o_ref.dtype)
        put_norm(best)

    @pl.when(j < nq)
    def _():
        normed_rope(tqc_ref, tqs_ref)

    @pl.when(jnp.logical_and(j >= nq, j < nq + nk))
    def _():
        normed_rope(tkc_ref, tks_ref)

    @pl.when(jnp.logical_and(j >= qb_lo, j < qb_hi))
    def _():
        scaled(qscale)

    @pl.when(jnp.logical_and(j >= qb_hi, j < kb_hi))
    def _():
        scaled(1.0)

    @pl.when(jnp.logical_or(jnp.logical_and(j >= nq + nk, j < qb_lo), j >= kb_hi))
    def _():
        o_ref[...] = acc.astype(o_ref.dtype)
        n_ref[...] = jnp.zeros(n_ref.shape, F32)


def _inproj(x2d, mod4, g_pre, w_in, tables, *, seq0, seq_len):
    n, d = x2d.shape
    cols = w_in.shape[1]
    a_width = A_HEADS * HEAD_DIM
    kv_width = A_KV_HEADS * HEAD_DIM
    b_width = B_HEADS * 2 * HEAD_DIM
    bm = min(INPROJ_BM, seq_len)
    bn = min(512, kv_width)
    assert seq_len % bm == 0 and a_width % bn == 0 and kv_width % bn == 0 and b_width % bn == 0
    nq, nk = a_width // bn, kv_width // bn
    qb_lo = (a_width + 2 * kv_width) // bn
    qb_hi = qb_lo + b_width // bn
    kb_hi = qb_hi + b_width // bn
    blocks_per_seq = seq_len // bm
    seq_of = lambda i: seq0 + i // blocks_per_seq
    tab_spec = pl.BlockSpec((bm, HEAD_DIM), lambda i, j: (i % blocks_per_seq, 0))
    kernel = functools.partial(_inproj_kernel, nq=nq, nk=nk, qb_lo=qb_lo, qb_hi=qb_hi, kb_hi=kb_hi,
                               heads_per_tile=bn // HEAD_DIM, qscale=HEAD_DIM ** -0.5)
    proj, nrm = pl.pallas_call(
        kernel,
        grid=(n // bm, cols // bn),
        in_specs=[pl.BlockSpec((bm, d), lambda i, j: (i, 0)),
                  pl.BlockSpec((None, None, 1, d), lambda i, j: (seq_of(i), 1, 0, 0)),
                  pl.BlockSpec((None, None, 1, d), lambda i, j: (seq_of(i), 0, 0, 0)),
                  pl.BlockSpec((1, d), lambda i, j: (0, 0)),
                  pl.BlockSpec((d, bn), lambda i, j: (0, j)),
                  tab_spec, tab_spec, tab_spec, tab_spec],
        out_specs=[pl.BlockSpec((bm, bn), lambda i, j: (i, j)),
                   pl.BlockSpec((None, None, 8, LANES), lambda i, j: (i, j, 0, 0))],
        out_shape=[jax.ShapeDtypeStruct((n, cols), BF16),
                   jax.ShapeDtypeStruct((n // bm, cols // bn, 8, LANES), F32)],
        scratch_shapes=[pltpu.VMEM((bm, d), BF16)],
        compiler_params=_cparams(("parallel", "arbitrary")),
        name="inproj",
    )(x2d, mod4, mod4, g_pre, w_in, *tables)
    t = jnp.max(nrm[:, :, 0, 0], axis=0)
    bound = lambda qs, ks: jnp.sqrt(jnp.max(t[qs]) * jnp.max(t[ks])) * NORM_SLACK
    c_a = bound(slice(0, nq), slice(nq, nq + nk))
    c_b = bound(slice(qb_lo, qb_hi), slice(qb_hi, kb_hi))
    return proj, c_a, c_b


def _rope_tables(seq_len, gain, scale):
    half = HEAD_DIM // 2
    n_rows = seq_len // GRID_W
    inv = np.float32(ROPE_THETA) ** (-np.arange(0, half, 2, dtype=np.float32) / np.float32(half))
    ar = (np.arange(n_rows, dtype=np.float32)[:, None] * inv[None, :]).astype(np.float64)
    ac = (np.arange(GRID_W, dtype=np.float32)[:, None] * inv[None, :]).astype(np.float64)
    per_row = lambda a: jnp.broadcast_to(jnp.asarray(a, F32)[:, None, :], (n_rows, GRID_W, half // 2))
    per_col = lambda a: jnp.broadcast_to(jnp.asarray(a, F32)[None, :, :], (n_rows, GRID_W, half // 2))
    cr, sr, cc, sc = per_row(np.cos(ar)), per_row(np.sin(ar)), per_col(np.cos(ac)), per_col(np.sin(ac))
    cos = jnp.concatenate([cr, cr, cc, cc], axis=-1).reshape(seq_len, HEAD_DIM)
    sin = jnp.concatenate([-sr, sr, -sc, sc], axis=-1).reshape(seq_len, HEAD_DIM)
    g = gain.astype(F32).reshape(HEAD_DIM)
    lane = jnp.arange(HEAD_DIM)
    partner = jnp.where((lane % 64) < 32, lane + 32, lane - 32)
    return cos * (g * scale)[None, :], sin * (g[partner] * scale)[None, :]


def _online_softmax_step(s, v, m_ref, l_ref, acc_ref):
    m_prev = m_ref[...]
    m_new = jnp.maximum(m_prev, jnp.max(s, axis=-1, keepdims=True))
    alpha = jnp.exp(m_prev - m_new)
    p = jnp.exp(s - m_new)
    l_ref[...] = alpha * l_ref[...] + jnp.sum(p, axis=-1, keepdims=True)
    acc_ref[...] = alpha * acc_ref[...] + jnp.dot(p.astype(BF16), v, preferred_element_type=F32)
    m_ref[...] = m_new


_NT = (((1,), (1,)), ((), ()))


def _attn_a_kernel(q_ref, k_ref, v_ref, o_ref, q4_ref, m_ref, l_ref, acc_ref, *, group, bq, bk, nkv):
    for h in range(group):
        q4_ref[h * bq:(h + 1) * bq, :] = q_ref[:, h * HEAD_DIM:(h + 1) * HEAD_DIM]
    m_ref[...] = jnp.full(m_ref.shape, -jnp.inf, F32)
    l_ref[...] = jnp.zeros(l_ref.shape, F32)
    acc_ref[...] = jnp.zeros(acc_ref.shape, F32)

    def body(j, carry):
        off = pl.multiple_of(j * bk, bk)
        k = k_ref[pl.ds(off, bk), :]
        v = v_ref[pl.ds(off, bk), :]
        s = lax.dot_general(q4_ref[...], k, _NT, preferred_element_type=F32)
        _online_softmax_step(s, v, m_ref, l_ref, acc_ref)
        return carry

    lax.fori_loop(0, nkv, body, 0)
    o = acc_ref[...] / l_ref[...]
    for h in range(group):
        o_ref[:, h * HEAD_DIM:(h + 1) * HEAD_DIM] = o[h * bq:(h + 1) * bq, :].astype(o_ref.dtype)


def _attn_a(proj, batch, seq_len):
    group = A_HEADS // A_KV_HEADS
    a_width = A_HEADS * HEAD_DIM
    kv_width = A_KV_HEADS * HEAD_DIM
    bq = min(ATTN_A_BQ, seq_len)
    bk = min(ATTN_A_BK, seq_len)
    qblocks = seq_len // bq
    gw = group * HEAD_DIM
    k_col0 = a_width // HEAD_DIM
    v_col0 = (a_width + kv_width) // HEAD_DIM
    rows = group * bq
    kernel = functools.partial(_attn_a_kernel, group=group, bq=bq, bk=bk, nkv=seq_len // bk)
    return pl.pallas_call(
        kernel,
        grid=(batch, A_KV_HEADS, qblocks),
        in_specs=[pl.BlockSpec((bq, gw), lambda b, g, i: (b * qblocks + i, g)),
                  pl.BlockSpec((seq_len, HEAD_DIM), lambda b, g, i: (b, k_col0 + g),
                               pipeline_mode=pl.Buffered(1)),
                  pl.BlockSpec((seq_len, HEAD_DIM), lambda b, g, i: (b, v_col0 + g),
                               pipeline_mode=pl.Buffered(1))],
        out_specs=pl.BlockSpec((bq, gw), lambda b, g, i: (b * qblocks + i, g)),
        out_shape=jax.ShapeDtypeStruct((batch * seq_len, a_width), BF16),
        scratch_shapes=[pltpu.VMEM((rows, HEAD_DIM), BF16),
                        pltpu.VMEM((rows, 1), F32),
                        pltpu.VMEM((rows, 1), F32),
                        pltpu.VMEM((rows, HEAD_DIM), F32)],
        compiler_params=_cparams(("parallel", "parallel", "parallel")),
        name="attn_a",
    )(proj, proj, proj)


def _attn_b_kernel(slope_ref, lam_ref, q_ref, k_ref, v_ref, subln_ref, o_ref, m_ref, l_ref, acc_ref,
                   *, bq, bk, nkv):
    h = pl.program_id(1)
    i = pl.program_id(2)
    slope = slope_ref[h]
    lam = lam_ref[0]
    q = q_ref[...]
    q1 = q[:, :HEAD_DIM]
    q2 = q[:, HEAD_DIM:]
    qpos = (i * bq + lax.broadcasted_iota(I32, (bq, 1), 0)).astype(F32)
    m_ref[...] = jnp.full(m_ref.shape, -jnp.inf, F32)
    l_ref[...] = jnp.zeros(l_ref.shape, F32)
    acc_ref[...] = jnp.zeros(acc_ref.shape, F32)

    def body(j, carry):
        off = pl.multiple_of(j * bk, bk)
        k = k_ref[pl.ds(off, bk), :]
        v = v_ref[pl.ds(off, bk), :]
        kpos = (j * bk + lax.broadcasted_iota(I32, (1, bk), 1)).astype(F32)
        bias = -slope * jnp.abs(qpos - kpos)
        s1 = lax.dot_general(q1, k[:, :HEAD_DIM], _NT, preferred_element_type=F32) + bias
        s2 = lax.dot_general(q2, k[:, HEAD_DIM:], _NT, preferred_element_type=F32) + bias
        _online_softmax_step(jnp.concatenate([s1, s2], axis=0), v, m_ref, l_ref, acc_ref)
        return carry

    lax.fori_loop(0, nkv, body, 0)
    o = acc_ref[...] / l_ref[...]
    o = o[:bq, :] - lam * o[bq:, :]
    r = lax.rsqrt(jnp.mean(o * o, axis=-1, keepdims=True) + RMS_EPS)
    o_ref[...] = ((o * r * subln_ref[...]) * (1.0 - LAM_INIT)).astype(o_ref.dtype)


def _attn_b(proj, slopes, lam, subln, batch, seq_len):
    a_width = A_HEADS * HEAD_DIM
    kv_width = A_KV_HEADS * HEAD_DIM
    b_width = B_HEADS * 2 * HEAD_DIM
    hw = 2 * HEAD_DIM
    bq = min(ATTN_B_BQ, seq_len)
    bk = min(ATTN_B_BK, seq_len)
    qblocks = seq_len // bq
    base = a_width + 2 * kv_width
    assert base % hw == 0
    q_col0, k_col0, v_col0 = base // hw, (base + b_width) // hw, (base + 2 * b_width) // hw
    kernel = functools.partial(_attn_b_kernel, bq=bq, bk=bk, nkv=seq_len // bk)
    grid_spec = pltpu.PrefetchScalarGridSpec(
        num_scalar_prefetch=2,
        grid=(batch, B_HEADS, qblocks),
        in_specs=[pl.BlockSpec((bq, hw), lambda b, h, i, *_: (b * qblocks + i, q_col0 + h)),
                  pl.BlockSpec((seq_len, hw), lambda b, h, i, *_: (b, k_col0 + h),
                               pipeline_mode=pl.Buffered(1)),
                  pl.BlockSpec((seq_len, hw), lambda b, h, i, *_: (b, v_col0 + h),
                               pipeline_mode=pl.Buffered(1)),
                  pl.BlockSpec((1, hw), lambda b, h, i, *_: (0, 0))],
        out_specs=pl.BlockSpec((bq, hw), lambda b, h, i, *_: (b * qblocks + i, h)),
        scratch_shapes=[pltpu.VMEM((2 * bq, 1), F32),
                        pltpu.VMEM((2 * bq, 1), F32),
                        pltpu.VMEM((2 * bq, hw), F32)],
    )
    return pl.pallas_call(
        kernel,
        grid_spec=grid_spec,
        out_shape=jax.ShapeDtypeStruct((batch * seq_len, b_width), BF16),
        compiler_params=_cparams(("parallel", "parallel", "parallel")),
        name="attn_b",
    )(slopes, lam, proj, proj, proj, subln)


def _lane_partial_sum(p):
    ps = p[:, 0:LANES]
    for t in range(1, p.shape[1] // LANES):
        ps = ps + p[:, t * LANES:(t + 1) * LANES]
    return ps


def _attn_a_fast_kernel(q_ref, k_ref, v_ref, o_ref, q4_ref, l_ref, acc_ref, *, group, bq, bk, nkv):
    for h in range(group):
        q4_ref[h * bq:(h + 1) * bq, :] = q_ref[:, h * HEAD_DIM:(h + 1) * HEAD_DIM]
    l_ref[...] = jnp.zeros(l_ref.shape, F32)
    acc_ref[...] = jnp.zeros(acc_ref.shape, F32)

    def body(j, carry):
        off = pl.multiple_of(j * bk, bk)
        k = k_ref[pl.ds(off, bk), :]
        v = v_ref[pl.ds(off, bk), :]
        p = jnp.exp(lax.dot_general(q4_ref[...], k, _NT, preferred_element_type=F32))
        l_ref[...] += _lane_partial_sum(p)
        acc_ref[...] += jnp.dot(p.astype(BF16), v, preferred_element_type=F32)
        return carry

    lax.fori_loop(0, nkv, body, 0)
    o = acc_ref[...] / jnp.sum(l_ref[...], axis=-1, keepdims=True)
    for h in range(group):
        o_ref[:, h * HEAD_DIM:(h + 1) * HEAD_DIM] = o[h * bq:(h + 1) * bq, :].astype(o_ref.dtype)


def _attn_a_fast(proj, batch, seq_len):
    group = A_HEADS // A_KV_HEADS
    a_width = A_HEADS * HEAD_DIM
    kv_width = A_KV_HEADS * HEAD_DIM
    bq = min(FAST_A_BQ, seq_len)
    bk = min(FAST_A_BK, seq_len)
    qblocks = seq_len // bq
    gw = group * HEAD_DIM
    k_col0 = a_width // HEAD_DIM
    v_col0 = (a_width + kv_width) // HEAD_DIM
    rows = group * bq
    kernel = functools.partial(_attn_a_fast_kernel, group=group, bq=bq, bk=bk, nkv=seq_len // bk)
    return pl.pallas_call(
        kernel,
        grid=(batch, A_KV_HEADS, qblocks),
        in_specs=[pl.BlockSpec((bq, gw), lambda b, g, i: (b * qblocks + i, g)),
                  pl.BlockSpec((seq_len, HEAD_DIM), lambda b, g, i: (b, k_col0 + g),
                               pipeline_mode=pl.Buffered(1)),
                  pl.BlockSpec((seq_len, HEAD_DIM), lambda b, g, i: (b, v_col0 + g),
                               pipeline_mode=pl.Buffered(1))],
        out_specs=pl.BlockSpec((bq, gw), lambda b, g, i: (b * qblocks + i, g)),
        out_shape=jax.ShapeDtypeStruct((batch * seq_len, a_width), BF16),
        scratch_shapes=[pltpu.VMEM((rows, HEAD_DIM), BF16),
                        pltpu.VMEM((rows, LANES), F32),
                        pltpu.VMEM((rows, HEAD_DIM), F32)],
        compiler_params=_cparams(("parallel", "parallel", "parallel")),
        name="attn_a_fast",
    )(proj, proj, proj)


def _attn_b_fast_kernel(slope_ref, lam_ref, reach_ref, q_ref, k_ref, v_ref, subln_ref, o_ref, l_ref, acc_ref,
                        p_ref, *, bq, bk, nkv):
    h = pl.program_id(1)
    i = pl.program_id(2)
    slope = slope_ref[h]
    lam = lam_ref[0]
    reach = reach_ref[h]
    q = q_ref[...]
    q1 = q[:, :HEAD_DIM]
    q2 = q[:, HEAD_DIM:]
    i0 = i * bq
    qpos = (i0 + lax.broadcasted_iota(I32, (bq, 1), 0)).astype(F32)
    l_ref[...] = jnp.zeros(l_ref.shape, F32)
    acc_ref[...] = jnp.zeros(acc_ref.shape, F32)
    jb_lo = jnp.maximum(i0 - reach, 0) // bk
    jb_hi = jnp.minimum((i0 + bq - 1 + reach) // bk + 1, nkv)

    odd = (jb_hi - jb_lo) % 2
    grow_hi = jnp.logical_and(odd == 1, jb_hi < nkv).astype(I32)
    jb_hi = jb_hi + grow_hi
    jb_lo = jb_lo - (odd - grow_hi)
    npair = (jb_hi - jb_lo) // 2

    def probs(j, slot):
        off = pl.multiple_of(j * bk, bk)
        k = k_ref[pl.ds(off, bk), :]
        kpos = (j * bk + lax.broadcasted_iota(I32, (1, bk), 1)).astype(F32)
        bias = -slope * jnp.abs(qpos - kpos)
        p1 = jnp.exp(lax.dot_general(q1, k[:, :HEAD_DIM], _NT, preferred_element_type=F32) + bias)
        p2 = jnp.exp(lax.dot_general(q2, k[:, HEAD_DIM:], _NT, preferred_element_type=F32) + bias)
        l_ref[:bq, :] += _lane_partial_sum(p1)
        l_ref[bq:, :] += _lane_partial_sum(p2)
        p_ref[slot, :bq, :] = p1.astype(BF16)
        p_ref[slot, bq:, :] = p2.astype(BF16)

    def weighted_values(j, slot):
        off = pl.multiple_of(j * bk, bk)
        acc_ref[...] += jnp.dot(p_ref[slot], v_ref[pl.ds(off, bk), :], preferred_element_type=F32)

    probs(jb_lo, 0)

    def body(t, carry):
        j = jb_lo + 2 * t
        weighted_values(j, 0)
        probs(j + 1, 1)
        weighted_values(j + 1, 1)
        probs(j + 2, 0)
        return carry

    lax.fori_loop(0, npair - 1, body, 0)
    weighted_values(jb_hi - 2, 0)
    probs(jb_hi - 1, 1)
    weighted_values(jb_hi - 1, 1)

    o = acc_ref[...] / jnp.sum(l_ref[...], axis=-1, keepdims=True)
    o = o[:bq, :] - lam * o[bq:, :]
    r = lax.rsqrt(jnp.mean(o * o, axis=-1, keepdims=True) + RMS_EPS)
    o_ref[...] = ((o * r * subln_ref[...]) * (1.0 - LAM_INIT)).astype(o_ref.dtype)


def _attn_b_fast(proj, slopes, lam, reach, subln, batch, seq_len):
    a_width = A_HEADS * HEAD_DIM
    kv_width = A_KV_HEADS * HEAD_DIM
    b_width = B_HEADS * 2 * HEAD_DIM
    hw = 2 * HEAD_DIM
    bq = min(FAST_B_BQ, seq_len)
    bk = min(FAST_B_BK, seq_len // 2)
    qblocks = seq_len // bq
    base = a_width + 2 * kv_width
    assert base % hw == 0 and (seq_len // bk) % 2 == 0
    q_col0, k_col0, v_col0 = base // hw, (base + b_width) // hw, (base + 2 * b_width) // hw
    kernel = functools.partial(_attn_b_fast_kernel, bq=bq, bk=bk, nkv=seq_len // bk)
    grid_spec = pltpu.PrefetchScalarGridSpec(
        num_scalar_prefetch=3,
        grid=(batch, B_HEADS, qblocks),
        in_specs=[pl.BlockSpec((bq, hw), lambda b, h, i, *_: (b * qblocks + i, q_col0 + h)),
                  pl.BlockSpec((seq_len, hw), lambda b, h, i, *_: (b, k_col0 + h),
                               pipeline_mode=pl.Buffered(1)),
                  pl.BlockSpec((seq_len, hw), lambda b, h, i, *_: (b, v_col0 + h),
                               pipeline_mode=pl.Buffered(1)),
                  pl.BlockSpec((1, hw), lambda b, h, i, *_: (0, 0))],
        out_specs=pl.BlockSpec((bq, hw), lambda b, h, i, *_: (b * qblocks + i, h)),
        scratch_shapes=[pltpu.VMEM((2 * bq, LANES), F32),
                        pltpu.VMEM((2 * bq, hw), F32),
                        pltpu.VMEM((2, 2 * bq, bk), BF16)],
    )
    return pl.pallas_call(
        kernel,
        grid_spec=grid_spec,
        out_shape=jax.ShapeDtypeStruct((batch * seq_len, b_width), BF16),
        compiler_params=_cparams(("parallel", "parallel", "parallel")),
        name="attn_b_fast",
    )(slopes, lam, reach, proj, proj, proj, subln)


def _attention(proj, c_a, c_b, slopes, lam, subln, batch, seq_len):
    oa = lax.cond(c_a <= SAFE_LOGIT,
                  lambda p: _attn_a_fast(p, batch, seq_len),
                  lambda p: _attn_a(p, batch, seq_len), proj)
    reach = jnp.clip(jnp.ceil((c_b - EXP_ZERO_BELOW) / slopes), 0, seq_len).astype(I32)
    ob = lax.cond(c_b <= SAFE_LOGIT,
                  lambda p: _attn_b_fast(p, slopes, lam, reach, subln, batch, seq_len),
                  lambda p: _attn_b(p, slopes, lam, subln, batch, seq_len), proj)
    return oa, ob


def _outmm_kernel(a1_ref, a2_ref, w1_ref, w2_ref, o_ref):
    o_ref[...] = (jnp.dot(a1_ref[...], w1_ref[...], preferred_element_type=F32)
                  + jnp.dot(a2_ref[...], w2_ref[...], preferred_element_type=F32))


def _outmm(oa, ob, w_out):
    n, ka = oa.shape
    kb = ob.shape[1]
    d = w_out.shape[1]
    bm = min(OUTMM_BM, n)
    bn = min(OUTMM_BN, d)
    assert ka == kb and ka % 16 == 0
    return pl.pallas_call(
        _outmm_kernel,
        grid=(n // bm, d // bn),
        in_specs=[pl.BlockSpec((bm, ka), lambda i, j: (i, 0)),
                  pl.BlockSpec((bm, kb), lambda i, j: (i, 0)),
                  pl.BlockSpec((ka, bn), lambda i, j: (0, j)),
                  pl.BlockSpec((kb, bn), lambda i, j: (1, j))],
        out_specs=pl.BlockSpec((bm, bn), lambda i, j: (i, j)),
        out_shape=jax.ShapeDtypeStruct((n, d), F32),
        compiler_params=_cparams(("parallel", "parallel")),
        name="outmm",
    )(oa, ob, w_out, w_out)


def _post_body(mix_ref, x_ref, gt_ref, sc_ref, sh_ref, gpost_ref, gpre_ref, wr_ref, br_ref,
               x1_ref, h2_ref, lg_ref):
    mix = mix_ref[...]
    r = lax.rsqrt(jnp.mean(mix * mix, axis=-1, keepdims=True) + RMS_EPS)
    x1 = x_ref[...] + gt_ref[...] * (mix * r * gpost_ref[...])
    x1_ref[...] = x1
    r2 = lax.rsqrt(jnp.mean(x1 * x1, axis=-1, keepdims=True) + RMS_EPS)
    h = (x1 * r2 * gpre_ref[...]) * (1.0 + sc_ref[...]) + sh_ref[...]
    hb = h.astype(BF16)
    half = h.shape[1] // 2
    hr = hb.astype(F32)
    lo = lax.bitcast_convert_type(hr[:, :half], U32) >> 16
    hi = lax.bitcast_convert_type(hr[:, half:], U32) & jnp.uint32(0xFFFF0000)
    h2_ref[...] = hi | lo
    lg_ref[...] = jnp.dot(hb, wr_ref[...], preferred_element_type=F32) + br_ref[...]


def _post_kernel(mixp_ref, mixs_ref, xp_ref, xs_ref, gt_ref, sc_ref, sh_ref, gpost_ref, gpre_ref,
                 wr_ref, br_ref, x1_ref, h2_ref, lg_ref, *, nbp):
    i = pl.program_id(0)
    rest = (gt_ref, sc_ref, sh_ref, gpost_ref, gpre_ref, wr_ref, br_ref, x1_ref, h2_ref, lg_ref)

    @pl.when(i < nbp)
    def _():
        _post_body(mixp_ref, xp_ref, *rest)

    @pl.when(i >= nbp)
    def _():
        _post_body(mixs_ref, xs_ref, *rest)


def _post(mix_p, mix_s, x_p, x_s, mod4, g_post, g_pre, w_router, b_router, *, seq_s, nseq_p):
    n_p, d = x_p.shape
    n_s = x_s.shape[0]
    bm = min(POST_BM, seq_s)
    assert n_p % bm == 0 and n_s % bm == 0 and seq_s % bm == 0
    nbp, nbs = n_p // bm, n_s // bm
    bps = seq_s // bm
    nt = n_p + n_s
    seq_of = lambda i: jnp.where(i < nbp, 0, nseq_p + (i - nbp) // bps)
    p_idx = lambda i: (jnp.minimum(i, nbp - 1), 0)
    s_idx = lambda i: (jnp.maximum(i - nbp, 0), 0)
    mod_spec = lambda which: pl.BlockSpec((None, None, 1, d), lambda i: (seq_of(i), which, 0, 0))
    row_spec = lambda c: pl.BlockSpec((bm, c), lambda i: (i, 0))
    return pl.pallas_call(
        functools.partial(_post_kernel, nbp=nbp),
        grid=(nbp + nbs,),
        in_specs=[pl.BlockSpec((bm, d), p_idx), pl.BlockSpec((bm, d), s_idx),
                  pl.BlockSpec((bm, d), p_idx), pl.BlockSpec((bm, d), s_idx),
                  mod_spec(2), mod_spec(4), mod_spec(3),
                  pl.BlockSpec((1, d), lambda i: (0, 0)),
                  pl.BlockSpec((1, d), lambda i: (0, 0)),
                  pl.BlockSpec((d, LANES), lambda i: (0, 0)),
                  pl.BlockSpec((1, LANES), lambda i: (0, 0))],
        out_specs=[row_spec(d), row_spec(d // 2), row_spec(LANES)],
        out_shape=[jax.ShapeDtypeStruct((nt, d), F32),
                   jax.ShapeDtypeStruct((nt, d // 2), U32),
                   jax.ShapeDtypeStruct((nt, LANES), F32)],
        compiler_params=_cparams(("parallel",)),
        name="post",
    )(mix_p, mix_s, x_p, x_s, mod4, mod4, mod4, g_post, g_pre, w_router, b_router)


def _route_kernel(lg_ref, idx_ref, w_ref, rank_ref, cnt_ref, carry_ref):
    i = pl.program_id(0)

    @pl.when(i == 0)
    def _():
        carry_ref[...] = jnp.zeros(carry_ref.shape, F32)

    lg = lg_ref[...]
    bt = lg.shape[0]
    lane = lax.broadcasted_iota(I32, lg.shape, 1)
    work = lg
    vals, idxs = [], []
    onehot = jnp.zeros(lg.shape, F32)
    for _ in range(TOP_K):
        mx = jnp.max(work, axis=-1, keepdims=True)
        ix = jnp.min(jnp.where(work == mx, lane, LANES), axis=-1, keepdims=True)
        sel = lane == ix
        vals.append(mx)
        idxs.append(ix)
        work = jnp.where(sel, -jnp.inf, work)
        onehot = onehot + sel.astype(F32)
    exps = [jnp.exp(v - vals[0]) for v in vals]
    denom = exps[0]
    for e in exps[1:]:
        denom = denom + e
    tri = (lax.broadcasted_iota(I32, (bt, bt), 0) > lax.broadcasted_iota(I32, (bt, bt), 1)).astype(BF16)
    prefix = jnp.dot(tri, onehot.astype(BF16), preferred_element_type=F32) + carry_ref[...]
    idx_out = jnp.zeros(lg.shape, I32)
    w_out = jnp.zeros(lg.shape, F32)
    rank_out = jnp.zeros(lg.shape, I32)
    for k in range(TOP_K):
        rk = jnp.sum(jnp.where(lane == idxs[k], prefix, 0.0), axis=-1, keepdims=True)
        idx_out = jnp.where(lane == k, idxs[k], idx_out)
        w_out = jnp.where(lane == k, exps[k] / denom, w_out)
        rank_out = jnp.where(lane == k, rk.astype(I32), rank_out)
    idx_ref[...] = idx_out
    w_ref[...] = w_out
    rank_ref[...] = rank_out
    carry_ref[...] = carry_ref[...] + jnp.sum(onehot, axis=0, keepdims=True)
    cnt_ref[...] = carry_ref[...]


def _route(logits):
    nt = logits.shape[0]
    bt = min(ROUTE_BT, nt)
    spec = pl.BlockSpec((bt, LANES), lambda i: (i, 0))
    return pl.pallas_call(
        _route_kernel,
        grid=(nt // bt,),
        in_specs=[spec],
        out_specs=[spec, spec, spec, pl.BlockSpec((1, LANES), lambda i: (0, 0))],
        out_shape=[jax.ShapeDtypeStruct((nt, LANES), I32),
                   jax.ShapeDtypeStruct((nt, LANES), F32),
                   jax.ShapeDtypeStruct((nt, LANES), I32),
                   jax.ShapeDtypeStruct((1, LANES), F32)],
        scratch_shapes=[pltpu.VMEM((1, LANES), F32)],
        compiler_params=_cparams(("arbitrary",)),
        name="route",
    )(logits)


def _dispatch_kernel(pad_ref, dest_ref, h_ref, xs_ref, zero_ref, sem, *, bt, n_experts, total):
    i = pl.program_id(0)

    def row_copy(src, r, d):
        return pltpu.make_async_copy(src.at[pl.ds(r, 1), :], xs_ref.at[pl.ds(d, 1), :], sem)

    @pl.when(i == 0)
    def _():
        zero_ref[...] = jnp.zeros(zero_ref.shape, zero_ref.dtype)

        def fill(lo, hi):
            def start(r, c):
                row_copy(zero_ref, 0, r).start()
                return c

            def wait(r, c):
                row_copy(zero_ref, 0, r).wait()
                return c

            lax.fori_loop(lo, hi, start, 0)
            lax.fori_loop(lo, hi, wait, 0)

        def per_expert(e, c):
            fill(pad_ref[0, e], pad_ref[1, e])
            return c

        lax.fori_loop(0, n_experts, per_expert, 0)
        fill(pad_ref[1, n_experts - 1], total)

    def start(t, c):
        for k in range(TOP_K):
            row_copy(h_ref, t, dest_ref[0, 0, t * TOP_K + k]).start(priority=k % 2)
        return c

    def wait(t, c):
        for k in range(TOP_K):
            row_copy(h_ref, t, dest_ref[0, 0, t * TOP_K + k]).wait()
        return c

    lax.fori_loop(0, bt, start, 0)
    lax.fori_loop(0, bt, wait, 0)


def _dispatch(pad_info, dest, h2p, total):
    nt, half = h2p.shape
    bt = min(DISPATCH_BT, nt)
    dest3 = dest.reshape(nt // bt, 1, bt * TOP_K)
    kernel = functools.partial(_dispatch_kernel, bt=bt, n_experts=N_EXPERTS, total=total)
    grid_spec = pltpu.PrefetchScalarGridSpec(
        num_scalar_prefetch=1,
        grid=(nt // bt,),
        in_specs=[pl.BlockSpec((1, 1, bt * TOP_K), lambda i, *_: (i, 0, 0), memory_space=pltpu.SMEM),
                  pl.BlockSpec((bt, half), lambda i, *_: (i, 0))],
        out_specs=pl.BlockSpec(memory_space=pl.ANY),
        scratch_shapes=[pltpu.VMEM((8, half), U32), pltpu.SemaphoreType.DMA(())],
    )
    return pl.pallas_call(
        kernel,
        grid_spec=grid_spec,
        out_shape=jax.ShapeDtypeStruct((total, half), U32),
        compiler_params=_cparams(("arbitrary",)),
        name="dispatch",
    )(pad_info, dest3, h2p)


def _unpack_rows(p):
    lo = lax.bitcast_convert_type(p << 16, F32)
    hi = lax.bitcast_convert_type(p & jnp.uint32(0xFFFF0000), F32)
    return lo.astype(BF16), hi.astype(BF16)


def _when_rows_valid(valid, bm, run):
    half_rows = bm // 2

    @pl.when(valid > half_rows)
    def _():
        run(bm)

    @pl.when(jnp.logical_and(valid > 0, valid <= half_rows))
    def _():
        run(half_rows)


def _segment_weights(meta, copies, n_outer, on_ready=None):
    be, bv, first, segidx, nxt, nseg = meta
    j = pl.program_id(0)
    i = pl.program_id(1)
    g = j * nseg[0] + segidx[i]
    slot = g % 2

    @pl.when(jnp.logical_and(bv[i] > 0, first[i] == 1))
    def _():
        @pl.when(g == 0)
        def _():
            for c in copies(be[i], j, slot):
                c.start()

        for c in copies(be[i], j, slot):
            c.wait()
        if on_ready is not None:
            on_ready(slot)
        last = (segidx[i] == nseg[0] - 1).astype(I32)

        @pl.when(jnp.logical_not(jnp.logical_and(last == 1, j == n_outer - 1)))
        def _():
            for c in copies(nxt[i], j + last, 1 - slot):
                c.start()

    return slot


def _gu_kernel(be, bv, first, segidx, nxt, nseg, xs_ref, w_hbm, bg_ref, bu_ref, o_ref, wbuf, sem, wb,
               *, nft, tf, f):
    def copies(e, j, slot):
        col = pl.multiple_of(j * tf, tf)
        return [pltpu.make_async_copy(w_hbm.at[e, :, pl.ds(col, tf)], wbuf.at[0], sem.at[0]),
                pltpu.make_async_copy(w_hbm.at[e, :, pl.ds(f + col, tf)], wbuf.at[1], sem.at[0])]

    def round_once(slot):
        wb[0] = wbuf[0].astype(BF16)
        wb[1] = wbuf[1].astype(BF16)

    _segment_weights((be, bv, first, segidx, nxt, nseg), copies, nft, on_ready=round_once)

    def run(rows):
        lo, hi = _unpack_rows(xs_ref[:rows, :])
        half = lo.shape[1]

        def proj(which, b_ref):
            return (jnp.dot(lo, wb[which, :half, :], preferred_element_type=F32)
                    + jnp.dot(hi, wb[which, half:, :], preferred_element_type=F32) + b_ref[...])

        g = jnp.minimum(proj(0, bg_ref), SWIGLU_LIMIT)
        u = jnp.clip(proj(1, bu_ref), -SWIGLU_LIMIT, SWIGLU_LIMIT)
        o_ref[:rows, :] = ((u + 1.0) * (g * jax.nn.sigmoid(SWIGLU_ALPHA * g))).astype(o_ref.dtype)

    _when_rows_valid(bv[pl.program_id(1)], xs_ref.shape[0], run)


def _gu(meta, xs, w_gu, b_gu):
    total, half = xs.shape
    d = 2 * half
    n_e, _, two_f = w_gu.shape
    f = two_f // 2
    bm = MOE_BM
    tf = min(GU_TF, f)
    nft = f // tf
    grid_spec = pltpu.PrefetchScalarGridSpec(
        num_scalar_prefetch=6,
        grid=(nft, total // bm),
        in_specs=[pl.BlockSpec((bm, half), lambda j, i, *_: (i, 0)),
                  pl.BlockSpec(memory_space=pl.ANY),
                  pl.BlockSpec((None, 1, tf), lambda j, i, be, *_: (be[i], 0, j)),
                  pl.BlockSpec((None, 1, tf), lambda j, i, be, *_: (be[i], 0, nft + j))],
        out_specs=pl.BlockSpec((bm, tf), lambda j, i, *_: (i, j)),
        scratch_shapes=[pltpu.VMEM((2, d, tf), F32), pltpu.SemaphoreType.DMA((2,)),
                        pltpu.VMEM((2, d, tf), BF16)],
    )
    return pl.pallas_call(
        functools.partial(_gu_kernel, nft=nft, tf=tf, f=f),
        grid_spec=grid_spec,
        out_shape=jax.ShapeDtypeStruct((total, f), BF16),
        compiler_params=_cparams(("arbitrary", "arbitrary")),
        name="moe_gu",
    )(*meta, xs, w_gu, b_gu.reshape(n_e, 1, two_f), b_gu.reshape(n_e, 1, two_f))


def _down_kernel(be, bv, first, segidx, nxt, nseg, a_ref, w_hbm, b_ref, o_ref, wbuf, sem, *, nnt, bn):
    def copies(e, j, slot):
        col = pl.multiple_of(j * bn, bn)
        return [pltpu.make_async_copy(w_hbm.at[e, :, pl.ds(col, bn)], wbuf.at[slot], sem.at[slot])]

    slot = _segment_weights((be, bv, first, segidx, nxt, nseg), copies, nnt)

    def run(rows):
        w = wbuf[slot].astype(BF16)
        o_ref[:rows, :] = jnp.dot(a_ref[:rows, :], w, preferred_element_type=F32) + b_ref[...]

    _when_rows_valid(bv[pl.program_id(1)], a_ref.shape[0], run)


def _down(meta, act, w_down, b_down):
    total, f = act.shape
    n_e, _, d = w_down.shape
    bm = MOE_BM
    bn = min(DOWN_BN, d)
    nnt = d // bn
    grid_spec = pltpu.PrefetchScalarGridSpec(
        num_scalar_prefetch=6,
        grid=(nnt, total // bm),
        in_specs=[pl.BlockSpec((bm, f), lambda j, i, *_: (i, 0)),
                  pl.BlockSpec(memory_space=pl.ANY),
                  pl.BlockSpec((None, 1, bn), lambda j, i, be, *_: (be[i], 0, j))],
        out_specs=pl.BlockSpec((bm, bn), lambda j, i, *_: (i, j)),
        scratch_shapes=[pltpu.VMEM((2, f, bn), F32), pltpu.SemaphoreType.DMA((2,))],
    )
    return pl.pallas_call(
        functools.partial(_down_kernel, nnt=nnt, bn=bn),
        grid_spec=grid_spec,
        out_shape=jax.ShapeDtypeStruct((total, d), F32),
        compiler_params=_cparams(("arbitrary", "arbitrary")),
        name="moe_down",
    )(*meta, act, w_down, b_down.reshape(n_e, 1, d))


def _final_kernel(dest_ref, w_ref, x1_ref, gt_ref, g_ref, eo_ref, y_ref, rows_ref, sem, *, bt):
    def row_copy(t, k):
        d = dest_ref[0, 0, t * TOP_K + k]
        return pltpu.make_async_copy(eo_ref.at[pl.ds(d, 1), :], rows_ref.at[k, pl.ds(t, 1), :], sem)

    def start(t, c):
        for k in range(TOP_K):
            row_copy(t, k).start()
        return c

    def wait(t, c):
        for k in range(TOP_K):
            row_copy(t, k).wait()
        return c

    lax.fori_loop(0, bt, start, 0)
    lax.fori_loop(0, bt, wait, 0)
    w = w_ref[...]
    f = w[:, 0:1] * rows_ref[0]
    for k in range(1, TOP_K):
        f = f + w[:, k:k + 1] * rows_ref[k]
    r = lax.rsqrt(jnp.mean(f * f, axis=-1, keepdims=True) + RMS_EPS)
    y_ref[...] = x1_ref[...] + gt_ref[...] * (f * r * g_ref[...])


def _final(dest, topw, x1_all, mod4, g_post, eo, *, row0, n, seq0, seq_len):
    d = x1_all.shape[1]
    bt = min(FINAL_BT, seq_len)
    assert row0 % bt == 0 and n % bt == 0 and seq_len % bt == 0
    b0 = row0 // bt
    bps = seq_len // bt
    nt = dest.shape[0]
    dest3 = dest.reshape(nt // bt, 1, bt * TOP_K)
    return pl.pallas_call(
        functools.partial(_final_kernel, bt=bt),
        grid=(n // bt,),
        in_specs=[pl.BlockSpec((1, 1, bt * TOP_K), lambda i: (b0 + i, 0, 0), memory_space=pltpu.SMEM),
                  pl.BlockSpec((bt, LANES), lambda i: (b0 + i, 0)),
                  pl.BlockSpec((bt, d), lambda i: (b0 + i, 0)),
                  pl.BlockSpec((None, None, 1, d), lambda i: (seq0 + i // bps, 5, 0, 0)),
                  pl.BlockSpec((1, d), lambda i: (0, 0)),
                  pl.BlockSpec(memory_space=pl.ANY)],
        out_specs=pl.BlockSpec((bt, d), lambda i: (i, 0)),
        out_shape=jax.ShapeDtypeStruct((n, d), F32),
        scratch_shapes=[pltpu.VMEM((TOP_K, bt, d), F32), pltpu.SemaphoreType.DMA(())],
        compiler_params=_cparams(("arbitrary",)),
        name="final",
    )(dest3, topw, x1_all, mod4, g_post, eo)


def kernel(x_prompt, x_sample, c_prompt, c_sample, w_ada, b_ada, g_pre_mix, g_post_mix, g_pre_ffn,
           g_post_ffn, w_in, w_out, q_norm, k_norm, lam_params, subln, w_router, b_router,
           w_gate_up, b_gate_up, w_down, b_down):
    bp, sp, d = x_prompt.shape
    bs, ss, _ = x_sample.shape
    assert bp == 1, "prompt group is a single sequence"
    n_p, n_s = bp * sp, bs * ss
    nt = n_p + n_s
    l = 0
    scale = HEAD_DIM ** -0.5

    w_in_b = w_in[l].astype(BF16)
    w_out_b = w_out[l].astype(BF16)
    w_r_b = jnp.zeros((d, LANES), BF16).at[:, :N_EXPERTS].set(w_router[l].astype(BF16))
    b_r = jnp.full((1, LANES), NEG_BIG, F32).at[0, :N_EXPERTS].set(b_router[l].astype(F32))
    row = lambda g: g[l].astype(F32).reshape(1, -1)
    lp = lam_params[l].astype(F32)
    lam = (jnp.exp(jnp.sum(lp[0] * lp[1])) - jnp.exp(jnp.sum(lp[2] * lp[3])) + LAM_INIT).reshape(1)
    slopes = 2.0 ** (-8.0 * jnp.arange(1, B_HEADS + 1, dtype=F32) / B_HEADS)

    n_seq = bp + bs
    c_pad = jnp.zeros((16, d), F32).at[:n_seq].set(jnp.concatenate([c_prompt, c_sample], axis=0).astype(F32))
    mod = _ada(c_pad, w_ada[l], b_ada[l].astype(F32))
    mod4 = mod.reshape(16, N_MOD, 1, d)

    xp2 = x_prompt.reshape(n_p, d)
    xs2 = x_sample.reshape(n_s, d)
    groups = ((xp2, bp, sp, 0), (xs2, bs, ss, bp))
    mixes = []
    for x2, batch, seq_len, seq0 in groups:
        tables = (*_rope_tables(seq_len, q_norm[l], scale), *_rope_tables(seq_len, k_norm[l], 1.0))
        proj, c_a, c_b = _inproj(x2, mod4, row(g_pre_mix), w_in_b, tables, seq0=seq0, seq_len=seq_len)
        oa, ob = _attention(proj, c_a, c_b, slopes, lam, row(subln), batch, seq_len)
        mixes.append(_outmm(oa, ob, w_out_b))

    x1_all, h2p, logits = _post(mixes[0], mixes[1], xp2, xs2, mod4, row(g_post_mix), row(g_pre_ffn),
                                w_r_b, b_r, seq_s=ss, nseq_p=bp)

    idx, topw, rank, cnt = _route(logits)
    counts = cnt[0, :N_EXPERTS].astype(I32)
    padded = ((counts + MOE_BM - 1) // MOE_BM) * MOE_BM
    ends = jnp.cumsum(padded)
    pad_start = ends - padded
    dest = (pad_start[idx[:, :TOP_K]] + rank[:, :TOP_K]).astype(I32)
    n_blocks = (nt * TOP_K) // MOE_BM + N_EXPERTS
    total = n_blocks * MOE_BM
    block_starts = jnp.arange(n_blocks, dtype=I32) * MOE_BM
    block_e = jnp.minimum(jnp.sum(block_starts[:, None] >= ends[None, :], axis=1), N_EXPERTS - 1).astype(I32)
    pad_info = jnp.stack([pad_start + counts, ends]).astype(I32)

    xs = _dispatch(pad_info, dest, h2p, total)
    block_valid = jnp.clip((pad_start + counts)[block_e] - block_starts, 0, MOE_BM).astype(I32)
    prev_e = jnp.concatenate([jnp.full((1,), -1, I32), block_e[:-1]])
    first = jnp.logical_and(block_valid > 0, block_e != prev_e)
    segidx = (jnp.cumsum(first.astype(I32)) - 1).astype(I32)
    nseg = jnp.sum(first.astype(I32)).reshape(1)
    seg_e = jnp.zeros((N_EXPERTS + 1,), I32).at[jnp.where(first, segidx, N_EXPERTS)].set(block_e)
    nxt_e = seg_e[(segidx + 1) % jnp.maximum(nseg[0], 1)]
    meta = (block_e, block_valid, first.astype(I32), jnp.maximum(segidx, 0), nxt_e.astype(I32),
            nseg.astype(I32))
    assert w_gate_up.shape[0] == 1 and w_down.shape[0] == 1, "single-layer stack"
    act = _gu(meta, xs, w_gate_up.reshape(w_gate_up.shape[1:]), b_gate_up[l].astype(F32))
    eo = _down(meta, act, w_down.reshape(w_down.shape[1:]), b_down[l].astype(F32))

    y_p = _final(dest, topw, x1_all, mod4, row(g_post_ffn), eo, row0=0, n=n_p, seq0=0, seq_len=sp)
    y_s = _final(dest, topw, x1_all, mod4, row(g_post_ffn), eo, row0=n_p, n=n_s, seq0=bp, seq_len=ss)
    return y_p.reshape(bp, sp, d), y_s.reshape(bs, ss, d)
```
